```python
import jax, jax.numpy as jnp
from jax import lax
import numpy as np

D_MODEL = 1024
BATCH = 8
SEQ = 4096
DEPTH = 4

CHUNK = 64
N_MIXERS = 2
N_ATTN_LAYERS = (DEPTH + 1) // 2
N_RNN_LAYERS = DEPTH // 2

ATTN_HEADS = 16
ATTN_HEAD_DIM = D_MODEL // ATTN_HEADS
ATTN_WIDTH = ATTN_HEADS * ATTN_HEAD_DIM
Q_BLOCK = 128

RNN_WIDTH = D_MODEL
RNN_BLOCK_WIDTH = 256
RNN_BLOCKS = RNN_WIDTH // RNN_BLOCK_WIDTH
CONV_WIDTH = 4
LRU_C = 8.0

DEEPNORM_ALPHA = (2.0 * DEPTH) ** 0.25
DEEPNORM_BETA = (8.0 * DEPTH) ** -0.25
LN_EPS = 1e-5

kernel_name = "fox_rglru_deepnorm_hybrid"


def layer_norm(x, g, b):
    xf = x.astype(jnp.float32)
    mu = jnp.mean(xf, axis=-1, keepdims=True)
    var = jnp.mean(jnp.square(xf - mu), axis=-1, keepdims=True)
    y = (xf - mu) * lax.rsqrt(var + LN_EPS) * g.astype(jnp.float32) + b.astype(jnp.float32)
    return y.astype(x.dtype)


def forgetting_attention(x, w_in, b_f, w_out):
    B, S, _ = x.shape
    H, Dh = ATTN_HEADS, ATTN_HEAD_DIM
    proj = jnp.einsum("bsd,de->bse", x, w_in)
    q, k, v, gate = jnp.split(proj[..., :4 * ATTN_WIDTH], 4, axis=-1)
    f_logit = proj[..., 4 * ATTN_WIDTH:].astype(jnp.float32) + b_f.astype(jnp.float32)
    cum = jnp.cumsum(jax.nn.log_sigmoid(f_logit), axis=1).transpose(0, 2, 1)

    def heads(t):
        return t.reshape(B, S, H, Dh).transpose(0, 2, 1, 3)

    q = heads(q) * (Dh ** -0.5)
    k = heads(k)
    v = heads(v)
    outs = []
    for blk in range(S // Q_BLOCK):
        q0 = blk * Q_BLOCK
        q1 = q0 + Q_BLOCK
        s = jnp.einsum("bhqd,bhkd->bhqk", q[:, :, q0:q1], k[:, :, :q1]).astype(jnp.float32)
        s = s + cum[:, :, q0:q1, None] - cum[:, :, None, :q1]
        causal = (q0 + jnp.arange(Q_BLOCK))[:, None] >= jnp.arange(q1)[None, :]
        p = jax.nn.softmax(jnp.where(causal, s, -jnp.inf), axis=-1)
        outs.append(jnp.einsum("bhqk,bhkd->bhqd", p.astype(v.dtype), v[:, :, :q1]))
    o = jnp.concatenate(outs, axis=2).transpose(0, 2, 1, 3).reshape(B, S, ATTN_WIDTH)
    return jnp.einsum("bse,ed->bsd", o * jax.nn.silu(gate), w_out)


def rglru_block(x, w_in, conv_w, conv_b, w_a, b_a, w_i, b_i, lam, w_out):
    B, S, _ = x.shape
    proj = jnp.einsum("bsd,de->bse", x, w_in)
    u, gate = jnp.split(proj, 2, axis=-1)
    u_pad = jnp.pad(u, ((0, 0), (CONV_WIDTH - 1, 0), (0, 0)))
    u = conv_b + sum(u_pad[:, tap:tap + S] * conv_w[tap] for tap in range(CONV_WIDTH))
    ub = u.reshape(B, S, RNN_BLOCKS, RNN_BLOCK_WIDTH)
    r = jax.nn.sigmoid(jnp.einsum("bsnc,ncd->bsnd", ub, w_a).reshape(B, S, RNN_WIDTH) + b_a)
    i = jax.nn.sigmoid(jnp.einsum("bsnc,ncd->bsnd", ub, w_i).reshape(B, S, RNN_WIDTH) + b_i)
    log_a = -LRU_C * r.astype(jnp.float32) * jax.nn.softplus(-lam.astype(jnp.float32))
    a = jnp.exp(log_a)
    bterm = jnp.sqrt(-jnp.expm1(2.0 * log_a)) * (i * u).astype(jnp.float32)

    def combine(left, right):
        a1, b1 = left
        a2, b2 = right
        return a1 * a2, a2 * b1 + b2

    _, h = lax.associative_scan(combine, (a, bterm), axis=1)
    y = h.astype(x.dtype) * jax.nn.silu(gate)
    return jnp.einsum("bse,ed->bsd", y, w_out)


def setup_inputs(seed: int = 0) -> dict:
    key = jax.random.key(seed)
    ks = jax.random.split(key, 16)
    nA, nR = N_ATTN_LAYERS, N_RNN_LAYERS
    f32 = jnp.float32
    x = jax.random.normal(ks[0], (BATCH, SEQ, D_MODEL), f32)
    ln_g = 1.0 + 0.02 * jax.random.normal(ks[1], (DEPTH, D_MODEL), f32)
    ln_b = 0.02 * jax.random.normal(ks[2], (DEPTH, D_MODEL), f32)
    attn_w_in = jax.random.normal(ks[3], (nA, D_MODEL, 4 * ATTN_WIDTH + ATTN_HEADS), f32) * D_MODEL ** -0.5
    attn_b_f = jax.random.uniform(ks[4], (nA, ATTN_HEADS), f32, 1.0, 4.0)
    attn_w_out = jax.random.normal(ks[5], (nA, ATTN_WIDTH, D_MODEL), f32) * (ATTN_WIDTH ** -0.5 * DEEPNORM_BETA)
    rnn_w_in = jax.random.normal(ks[6], (nR, D_MODEL, 2 * RNN_WIDTH), f32) * D_MODEL ** -0.5
    rnn_conv_w = jax.random.normal(ks[7], (nR, CONV_WIDTH, RNN_WIDTH), f32) * CONV_WIDTH ** -0.5
    rnn_conv_b = 0.02 * jax.random.normal(ks[8], (nR, RNN_WIDTH), f32)
    rnn_w_a = jax.random.normal(ks[9], (nR, RNN_BLOCKS, RNN_BLOCK_WIDTH, RNN_BLOCK_WIDTH), f32) * RNN_BLOCK_WIDTH ** -0.5
    rnn_b_a = 0.02 * jax.random.normal(ks[10], (nR, RNN_WIDTH), f32)
    rnn_w_i = jax.random.normal(ks[11], (nR, RNN_BLOCKS, RNN_BLOCK_WIDTH, RNN_BLOCK_WIDTH), f32) * RNN_BLOCK_WIDTH ** -0.5
    rnn_b_i = 0.02 * jax.random.normal(ks[12], (nR, RNN_WIDTH), f32)
    a0 = jax.random.uniform(ks[13], (nR, RNN_WIDTH), f32, 0.9, 0.999)
    rnn_lambda = jnp.log(a0) - jnp.log1p(-a0)
    rnn_w_out = jax.random.normal(ks[14], (nR, RNN_WIDTH, D_MODEL), f32) * (RNN_WIDTH ** -0.5 * DEEPNORM_BETA)
    return {
        "x": x, "ln_g": ln_g, "ln_b": ln_b,
        "attn_w_in": attn_w_in, "attn_b_f": attn_b_f, "attn_w_out": attn_w_out,
        "rnn_w_in": rnn_w_in, "rnn_conv_w": rnn_conv_w, "rnn_conv_b": rnn_conv_b,
        "rnn_w_a": rnn_w_a, "rnn_b_a": rnn_b_a, "rnn_w_i": rnn_w_i, "rnn_b_i": rnn_b_i,
        "rnn_lambda": rnn_lambda, "rnn_w_out": rnn_w_out,
    }


def reference(x, ln_g, ln_b, attn_w_in, attn_b_f, attn_w_out, rnn_w_in, rnn_conv_w,
              rnn_conv_b, rnn_w_a, rnn_b_a, rnn_w_i, rnn_b_i, rnn_lambda, rnn_w_out):
    for layer in range(DEPTH):
        idx = layer // N_MIXERS
        if layer % N_MIXERS == 0:
            h = forgetting_attention(x, attn_w_in[idx], attn_b_f[idx], attn_w_out[idx])
        else:
            h = rglru_block(x, rnn_w_in[idx], rnn_conv_w[idx], rnn_conv_b[idx],
                            rnn_w_a[idx], rnn_b_a[idx], rnn_w_i[idx], rnn_b_i[idx],
                            rnn_lambda[idx], rnn_w_out[idx])
        x = layer_norm(DEEPNORM_ALPHA * x + h, ln_g[layer], ln_b[layer])
    return x
```

```python
import functools

import jax
import jax.numpy as jnp
import numpy as np
from jax import lax
from jax.experimental import pallas as pl
from jax.experimental.pallas import tpu as pltpu

F32 = jnp.float32
BF16 = jnp.bfloat16

D_MODEL = 1024
DEPTH = 4
N_MIXERS = 2
ATTN_HEADS = 16
ATTN_HEAD_DIM = 64
ATTN_WIDTH = ATTN_HEADS * ATTN_HEAD_DIM
RNN_WIDTH = D_MODEL
RNN_BLOCK_WIDTH = 256
RNN_BLOCKS = RNN_WIDTH // RNN_BLOCK_WIDTH
CONV_WIDTH = 4
LRU_C = 8.0
DEEPNORM_ALPHA = (2.0 * DEPTH) ** 0.25
LN_EPS = 1e-5

LANES = 128
SUBLANES = 8
HEADS_PER_BLOCK = LANES // ATTN_HEAD_DIM
HEAD_BLOCKS = ATTN_HEADS // HEADS_PER_BLOCK
AUG_STRIDE = 8
CUM_PARTS = 3

VMEM_LIMIT = 56 * 1024 * 1024

PROJ_ROWS = 512
OUT_ROWS = 512
RNN_ROWS = 256
ATTN_TQ = 256
ATTN_TK = 256


def _softplus(x):
    return jnp.maximum(x, 0.0) + jnp.log1p(jnp.exp(-jnp.abs(x)))


def _log_sigmoid(x):
    return jnp.minimum(x, 0.0) - jnp.log1p(jnp.exp(-jnp.abs(x)))


def _sigmoid(x):
    return 1.0 / (1.0 + jnp.exp(-x))


def _silu(x):
    return x * _sigmoid(x)


def _split_bf16(x):
    hi = x.astype(BF16)
    r1 = x - hi.astype(F32)
    mid = r1.astype(BF16)
    lo = (r1 - mid.astype(F32)).astype(BF16)
    return hi, mid, lo


def _deepnorm_ln(x, h, g, b):
    z = DEEPNORM_ALPHA * x + h
    mu = jnp.mean(z, axis=-1, keepdims=True)
    zc = z - mu
    var = jnp.mean(zc * zc, axis=-1, keepdims=True)
    return zc * lax.rsqrt(var + LN_EPS) * g + b


def _attn_proj_kernel(x_ref, w_ref, wf_ref, bf_ref, tri_ref, eq_ref, ek_ref, oq_ref, ok_ref,
                      q_ref, k_ref, v_ref, g_ref, qa_ref, ka_ref, carry_ref, *, tm):
    si = pl.program_id(1)

    @pl.when(si == 0)
    def _():
        carry_ref[...] = jnp.zeros_like(carry_ref)

    xb = x_ref[0].astype(BF16)
    w = ATTN_WIDTH
    q = jnp.dot(xb, w_ref[:, 0:w], preferred_element_type=F32)
    q_ref[0] = (q * (ATTN_HEAD_DIM ** -0.5)).astype(BF16)
    k_ref[0] = jnp.dot(xb, w_ref[:, w:2 * w], preferred_element_type=F32).astype(BF16)
    v_ref[0] = jnp.dot(xb, w_ref[:, 2 * w:3 * w], preferred_element_type=F32).astype(BF16)
    g_ref[0] = jnp.dot(xb, w_ref[:, 3 * w:4 * w], preferred_element_type=F32).astype(BF16)

    f = jnp.dot(xb, wf_ref[...], preferred_element_type=F32) + bf_ref[...]
    ls = _log_sigmoid(f)
    tri = tri_ref[...]
    cum = carry_ref[0:1, :]
    for part in _split_bf16(ls):
        cum = cum + jnp.dot(tri, part, preferred_element_type=F32)
    carry_ref[0:1, :] = cum[tm - 1:tm, :]

    parts = jnp.concatenate(_split_bf16(cum), axis=1)
    qa = jnp.dot(parts, eq_ref[...], preferred_element_type=F32) + oq_ref[...]
    ka = jnp.dot(parts, ek_ref[...], preferred_element_type=F32) + ok_ref[...]
    qa_ref[0] = qa.astype(BF16)
    ka_ref[0] = ka.astype(BF16)


def _bias_expanders():
    eq = np.zeros((CUM_PARTS * LANES, HEAD_BLOCKS * LANES), np.float32)
    ek = np.zeros_like(eq)
    oq = np.zeros((1, HEAD_BLOCKS * LANES), np.float32)
    ok = np.zeros_like(oq)
    for h in range(ATTN_HEADS):
        base = (h // HEADS_PER_BLOCK) * LANES + (h % HEADS_PER_BLOCK) * AUG_STRIDE
        for p in range(CUM_PARTS):
            eq[p * LANES + h, base + p] = 1.0
            ok[0, base + p] = 1.0
            ek[p * LANES + h, base + CUM_PARTS + p] = -1.0
            oq[0, base + CUM_PARTS + p] = 1.0
    return (jnp.asarray(eq, BF16), jnp.asarray(ek, BF16), jnp.asarray(oq, F32), jnp.asarray(ok, F32))


def _attn_proj(x, w_main, w_f, b_f):
    B, S, D = x.shape
    tm = PROJ_ROWS
    tri = jnp.asarray(np.tril(np.ones((tm, tm), np.float32)), BF16)
    eq, ek, oq, ok = _bias_expanders()
    const = lambda b, s: (0, 0)
    row_blk = lambda b, s: (b, s, 0)
    wide = pl.BlockSpec((1, tm, ATTN_WIDTH), row_blk)
    aug = pl.BlockSpec((1, tm, HEAD_BLOCKS * LANES), row_blk)
    act = jax.ShapeDtypeStruct((B, S, ATTN_WIDTH), BF16)
    augs = jax.ShapeDtypeStruct((B, S, HEAD_BLOCKS * LANES), BF16)
    return pl.pallas_call(
        functools.partial(_attn_proj_kernel, tm=tm),
        grid=(B, S // tm),
        in_specs=[
            pl.BlockSpec((1, tm, D), row_blk),
            pl.BlockSpec(w_main.shape, const),
            pl.BlockSpec(w_f.shape, const),
            pl.BlockSpec(b_f.shape, const),
            pl.BlockSpec(tri.shape, const),
            pl.BlockSpec(eq.shape, const),
            pl.BlockSpec(ek.shape, const),
            pl.BlockSpec(oq.shape, const),
            pl.BlockSpec(ok.shape, const),
        ],
        out_specs=[wide, wide, wide, wide, aug, aug],
        out_shape=[act, act, act, act, augs, augs],
        scratch_shapes=[pltpu.VMEM((SUBLANES, LANES), F32)],
        compiler_params=pltpu.CompilerParams(
            dimension_semantics=("arbitrary", "arbitrary"), vmem_limit_bytes=VMEM_LIMIT),
        name="attn_proj",
    )(x, w_main, w_f, b_f, tri, eq, ek, oq, ok)


def _flash_kernel(q_ref, qa_ref, k_ref, ka_ref, v_ref, o_ref, *, tq, tk):
    qi = pl.program_id(2)
    q = q_ref[0]
    qa = qa_ref[0]
    lane = lax.broadcasted_iota(jnp.int32, (tq, LANES), 1)
    row = lax.broadcasted_iota(jnp.int32, (tq, tk), 0)
    col = lax.broadcasted_iota(jnp.int32, (tq, tk), 1)
    causal = col <= row
    zero = jnp.zeros_like(q)

    def block(kb, carry, qq, masked):
        m, l, acc = carry
        ks = pl.multiple_of(kb * tk, tk)
        kk = jnp.concatenate([k_ref[0, pl.ds(ks, tk), :], ka_ref[0, pl.ds(ks, tk), :]], axis=1)
        s = lax.dot_general(qq, kk, (((1,), (1,)), ((), ())), preferred_element_type=F32)
        if masked:
            s = jnp.where(causal, s, -jnp.inf)
        m_new = jnp.maximum(m, jnp.max(s, axis=1, keepdims=True))
        alpha = jnp.exp(m - m_new)
        p = jnp.exp(s - m_new)
        l = alpha * l + jnp.sum(p, axis=1, keepdims=True)
        pv = jnp.dot(p.astype(BF16), v_ref[0, pl.ds(ks, tk), :], preferred_element_type=F32)
        return m_new, l, alpha * acc + pv

    outs = []
    for j in range(HEADS_PER_BLOCK):
        head_lanes = (lane >= j * ATTN_HEAD_DIM) & (lane < (j + 1) * ATTN_HEAD_DIM)
        bias_lanes = (lane >= j * AUG_STRIDE) & (lane < (j + 1) * AUG_STRIDE)
        qq = jnp.concatenate([jnp.where(head_lanes, q, zero), jnp.where(bias_lanes, qa, zero)], axis=1)
        init = (jnp.full((tq, 1), -jnp.inf, F32), jnp.zeros((tq, 1), F32), jnp.zeros((tq, LANES), F32))
        carry = lax.fori_loop(0, qi, functools.partial(block, qq=qq, masked=False), init)
        _, l, acc = block(qi, carry, qq, True)
        outs.append(acc / l)
    o_ref[0] = jnp.where(lane < ATTN_HEAD_DIM, outs[0], outs[1]).astype(BF16)


def _flash(q, qa, k, ka, v):
    B, S, _ = q.shape
    tq, tk = ATTN_TQ, ATTN_TK
    assert tq == tk
    q_blk = pl.BlockSpec((1, tq, LANES), lambda b, h, i: (b, i, h))
    kv_blk = pl.BlockSpec((1, S, LANES), lambda b, h, i: (b, 0, h))
    return pl.pallas_call(
        functools.partial(_flash_kernel, tq=tq, tk=tk),
        grid=(B, HEAD_BLOCKS, S // tq),
        in_specs=[q_blk, q_blk, kv_blk, kv_blk, kv_blk],
        out_specs=q_blk,
        out_shape=jax.ShapeDtypeStruct((B, S, ATTN_WIDTH), BF16),
        compiler_params=pltpu.CompilerParams(
            dimension_semantics=("arbitrary", "arbitrary", "arbitrary"), vmem_limit_bytes=VMEM_LIMIT),
        name="fox_flash",
    )(q, qa, k, ka, v)


def _attn_out_kernel(o_ref, g_ref, x_ref, w_ref, lg_ref, lb_ref, y_ref):
    y = o_ref[0].astype(F32) * _silu(g_ref[0].astype(F32))
    h = jnp.dot(y.astype(BF16), w_ref[...], preferred_element_type=F32)
    y_ref[0] = _deepnorm_ln(x_ref[0], h, lg_ref[...], lb_ref[...])


def _attn_out(o, gate, x, w_out, ln_g, ln_b):
    B, S, D = x.shape
    tm = OUT_ROWS
    const = lambda b, s: (0, 0)
    row_blk = lambda b, s: (b, s, 0)
    return pl.pallas_call(
        _attn_out_kernel,
        grid=(B, S // tm),
        in_specs=[
            pl.BlockSpec((1, tm, ATTN_WIDTH), row_blk),
            pl.BlockSpec((1, tm, ATTN_WIDTH), row_blk),
            pl.BlockSpec((1, tm, D), row_blk),
            pl.BlockSpec(w_out.shape, const),
            pl.BlockSpec(ln_g.shape, const),
            pl.BlockSpec(ln_b.shape, const),
        ],
        out_specs=pl.BlockSpec((1, tm, D), row_blk),
        out_shape=jax.ShapeDtypeStruct((B, S, D), F32),
        compiler_params=pltpu.CompilerParams(
            dimension_semantics=("arbitrary", "arbitrary"), vmem_limit_bytes=VMEM_LIMIT),
        name="attn_out",
    )(o, gate, x, w_out, ln_g, ln_b)


def _rnn_kernel(x_ref, win_ref, cw_ref, cb_ref, wa_ref, ba_ref, wi_ref, bi_ref, lam_ref, wout_ref,
                lg_ref, lb_ref, y_ref, tail_ref, h_ref, *, tm):
    si = pl.program_id(1)

    @pl.when(si == 0)
    def _():
        tail_ref[...] = jnp.zeros_like(tail_ref)
        h_ref[...] = jnp.zeros_like(h_ref)

    w = RNN_WIDTH
    x = x_ref[0]
    xb = x.astype(BF16)
    u = jnp.dot(xb, win_ref[:, 0:w], preferred_element_type=F32)
    gate = jnp.dot(xb, win_ref[:, w:2 * w], preferred_element_type=F32)

    row8 = lax.broadcasted_iota(jnp.int32, (SUBLANES, w), 0)
    tail = tail_ref[...]
    conv = cb_ref[...] + u * cw_ref[CONV_WIDTH - 1:CONV_WIDTH, :]
    for back in range(1, CONV_WIDTH):
        shifted = pltpu.roll(u, back, 0)
        head = jnp.where(row8 < back, pltpu.roll(tail, back, 0), shifted[0:SUBLANES])
        shifted = jnp.concatenate([head, shifted[SUBLANES:]], axis=0)
        tap = CONV_WIDTH - 1 - back
        conv = conv + shifted * cw_ref[tap:tap + 1, :]
    tail_ref[...] = u[tm - SUBLANES:tm]

    cbf = conv.astype(BF16)
    bw = RNN_BLOCK_WIDTH
    ra = jnp.concatenate(
        [jnp.dot(cbf[:, n * bw:(n + 1) * bw], wa_ref[n], preferred_element_type=F32) for n in range(RNN_BLOCKS)],
        axis=1)
    ri = jnp.concatenate(
        [jnp.dot(cbf[:, n * bw:(n + 1) * bw], wi_ref[n], preferred_element_type=F32) for n in range(RNN_BLOCKS)],
        axis=1)
    r = _sigmoid(ra + ba_ref[...])
    i = _sigmoid(ri + bi_ref[...])
    log_a = (-LRU_C * _softplus(-lam_ref[...])) * r
    a = jnp.exp(log_a)
    b = jnp.sqrt(jnp.tanh(-log_a) * (a * a + 1.0)) * (i * conv)

    rowm = lax.broadcasted_iota(jnp.int32, (tm, w), 0) % SUBLANES
    for shift in (1, 2, 4):
        ap = pltpu.roll(a, shift, 0)
        bp = pltpu.roll(b, shift, 0)
        take = rowm >= shift
        b = jnp.where(take, a * bp + b, b)
        a = jnp.where(take, a * ap, a)
    carry = h_ref[0:1, :]
    hs = []
    for g in range(tm // SUBLANES):
        lo = g * SUBLANES
        hg = a[lo:lo + SUBLANES] * carry + b[lo:lo + SUBLANES]
        carry = hg[SUBLANES - 1:SUBLANES]
        hs.append(hg)
    h_ref[0:1, :] = carry
    h = jnp.concatenate(hs, axis=0)

    y = h * _silu(gate)
    out = jnp.dot(y.astype(BF16), wout_ref[...], preferred_element_type=F32)
    y_ref[0] = _deepnorm_ln(x, out, lg_ref[...], lb_ref[...])


def _rnn_layer(x, w_in, conv_w, conv_b, w_a, b_a, w_i, b_i, lam, w_out, ln_g, ln_b):
    B, S, D = x.shape
    tm = RNN_ROWS
    row_blk = lambda b, s: (b, s, 0)

    def const(arr):
        nd = arr.ndim
        return pl.BlockSpec(arr.shape, lambda b, s: (0,) * nd)

    consts = [w_in, conv_w, conv_b, w_a, b_a, w_i, b_i, lam, w_out, ln_g, ln_b]
    return pl.pallas_call(
        functools.partial(_rnn_kernel, tm=tm),
        grid=(B, S // tm),
        in_specs=[pl.BlockSpec((1, tm, D), row_blk)] + [const(c) for c in consts],
        out_specs=pl.BlockSpec((1, tm, D), row_blk),
        out_shape=jax.ShapeDtypeStruct((B, S, D), F32),
        scratch_shapes=[pltpu.VMEM((SUBLANES, RNN_WIDTH), F32), pltpu.VMEM((SUBLANES, RNN_WIDTH), F32)],
        compiler_params=pltpu.CompilerParams(
            dimension_semantics=("arbitrary", "arbitrary"), vmem_limit_bytes=VMEM_LIMIT),
        name="rglru_layer",
    )(x, *consts)


def _row(v):
    return v.reshape(1, -1).astype(F32)


def kernel(x, ln_g, ln_b, attn_w_in, attn_b_f, attn_w_out, rnn_w_in, rnn_conv_w, rnn_conv_b,
           rnn_w_a, rnn_b_a, rnn_w_i, rnn_b_i, rnn_lambda, rnn_w_out):
    assert x.shape[-1] == D_MODEL
    for layer in range(DEPTH):
        idx = layer // N_MIXERS
        g, b = _row(ln_g[layer]), _row(ln_b[layer])
        if layer % N_MIXERS == 0:
            w_in = attn_w_in[idx]
            w_main = w_in[:, :4 * ATTN_WIDTH].astype(BF16)
            w_f = jnp.pad(w_in[:, 4 * ATTN_WIDTH:], ((0, 0), (0, LANES - ATTN_HEADS))).astype(BF16)
            b_f = jnp.pad(attn_b_f[idx].astype(F32), (0, LANES - ATTN_HEADS)).reshape(1, LANES)
            q, k, v, gate, qa, ka = _attn_proj(x, w_main, w_f, b_f)
            o = _flash(q, qa, k, ka, v)
            x = _attn_out(o, gate, x, attn_w_out[idx].astype(BF16), g, b)
        else:
            x = _rnn_layer(
                x, rnn_w_in[idx].astype(BF16), rnn_conv_w[idx].astype(F32), _row(rnn_conv_b[idx]),
                rnn_w_a[idx].astype(BF16), _row(rnn_b_a[idx]), rnn_w_i[idx].astype(BF16), _row(rnn_b_i[idx]),
                _row(rnn_lambda[idx]), rnn_w_out[idx].astype(BF16), g, b)
    return x
```

```python
import functools

import jax
import jax.numpy as jnp
import numpy as np
from jax import lax
from jax.experimental import pallas as pl
from jax.experimental.pallas import tpu as pltpu

F32 = jnp.float32
BF16 = jnp.bfloat16

D_MODEL = 1024
DEPTH = 4
N_MIXERS = 2
ATTN_HEADS = 16
ATTN_HEAD_DIM = 64
ATTN_WIDTH = ATTN_HEADS * ATTN_HEAD_DIM
RNN_WIDTH = D_MODEL
RNN_BLOCK_WIDTH = 256
RNN_BLOCKS = RNN_WIDTH // RNN_BLOCK_WIDTH
CONV_WIDTH = 4
LRU_C = 8.0
DEEPNORM_ALPHA = (2.0 * DEPTH) ** 0.25
LN_EPS = 1e-5
LOG2_E = 1.4426950408889634

LANES = 128
SUBLANES = 8
HEADS_PER_BLOCK = LANES // ATTN_HEAD_DIM
HEAD_BLOCKS = ATTN_HEADS // HEADS_PER_BLOCK
AUG_STRIDE = 8
CUM_PARTS = 3
ONES_LANE = ATTN_HEADS

VMEM_LIMIT = 56 * 1024 * 1024

PROJ_ROWS = 512
OUT_ROWS = 512
RNN_ROWS = 256
ATTN_TK = 256
ATTN_TQ = 2 * ATTN_TK


def _softplus(x):
    return jnp.maximum(x, 0.0) + jnp.log1p(jnp.exp(-jnp.abs(x)))


def _log_sigmoid(x):
    return jnp.minimum(x, 0.0) - jnp.log1p(jnp.exp(-jnp.abs(x)))


def _sigmoid(x):
    return 1.0 / (1.0 + jnp.exp(-x))


def _silu(x):
    return x * _sigmoid(x)


def _split_bf16(x):
    hi = x.astype(BF16)
    r1 = x - hi.astype(F32)
    mid = r1.astype(BF16)
    lo = (r1 - mid.astype(F32)).astype(BF16)
    return hi, mid, lo


def _deepnorm_ln(x, h, g, b):
    z = DEEPNORM_ALPHA * x + h
    mu = jnp.mean(z, axis=-1, keepdims=True)
    zc = z - mu
    var = jnp.mean(zc * zc, axis=-1, keepdims=True)
    return zc * lax.rsqrt(var + LN_EPS) * g + b


def _dot_nt(a, b):
    return lax.dot_general(a, b, (((1,), (1,)), ((), ())), preferred_element_type=F32)


def _attn_proj_kernel(x_ref, wk_ref, wg_ref, wqt_ref, wvt_ref, wf_ref, bf_ref, tri_ref, eqt_ref, ek_ref,
                      qt_ref, qat_ref, k_ref, ka_ref, vt_ref, g_ref, carry_ref, *, tm, tk):
    si = pl.program_id(1)

    @pl.when(si == 0)
    def _():
        carry_ref[...] = jnp.zeros_like(carry_ref)

    xb = x_ref[0].astype(BF16)
    k_ref[0] = jnp.dot(xb, wk_ref[...], preferred_element_type=F32).astype(BF16)
    g_ref[0] = jnp.dot(xb, wg_ref[...], preferred_element_type=F32).astype(BF16)
    qt_ref[0] = (_dot_nt(wqt_ref[...], xb) * (ATTN_HEAD_DIM ** -0.5 * LOG2_E)).astype(BF16)
    vt = _dot_nt(wvt_ref[...], xb).astype(BF16)
    for hb in range(HEAD_BLOCKS):
        for c in range(tm // tk):
            vt_ref[0, hb, c] = vt[hb * LANES:(hb + 1) * LANES, c * tk:(c + 1) * tk]

    f = jnp.dot(xb, wf_ref[...], preferred_element_type=F32) + bf_ref[...]
    ls = _log_sigmoid(f)
    tri = tri_ref[...]
    cum = carry_ref[0:1, :]
    for part in _split_bf16(ls):
        cum = cum + jnp.dot(tri, part, preferred_element_type=F32)
    carry_ref[0:1, :] = cum[tm - 1:tm, :]

    hi, mid, lo = _split_bf16(cum * LOG2_E)
    lane = lax.broadcasted_iota(jnp.int32, hi.shape, 1)
    hi = jnp.where(lane == ONES_LANE, jnp.ones_like(hi), hi)
    parts = jnp.concatenate([hi, mid, lo], axis=1)
    qat_ref[0] = _dot_nt(eqt_ref[...], parts).astype(BF16)
    ka_ref[0] = jnp.dot(parts, ek_ref[...], preferred_element_type=F32).astype(BF16)


def _bias_expanders():
    eq = np.zeros((CUM_PARTS * LANES, HEAD_BLOCKS * LANES), np.float32)
    ek = np.zeros_like(eq)
    for h in range(ATTN_HEADS):
        base = (h // HEADS_PER_BLOCK) * LANES + (h % HEADS_PER_BLOCK) * AUG_STRIDE
        for p in range(CUM_PARTS):
            eq[p * LANES + h, base + p] = 1.0
            ek[ONES_LANE, base + p] = 1.0
            ek[p * LANES + h, base + CUM_PARTS + p] = -1.0
            eq[ONES_LANE, base + CUM_PARTS + p] = 1.0
    return jnp.asarray(eq.T, BF16), jnp.asarray(ek, BF16)


def _attn_proj(x, w_k, w_g, w_qt, w_vt, w_f, b_f):
    B, S, D = x.shape
    tm, tk = PROJ_ROWS, ATTN_TK
    tri = jnp.asarray(np.tril(np.ones((tm, tm), np.float32)), BF16)
    eqt, ek = _bias_expanders()
    consts = [w_k, w_g, w_qt, w_vt, w_f, b_f, tri, eqt, ek]
    const = lambda b, s: (0, 0)
    row_blk = lambda b, s: (b, s, 0)
    col_blk = lambda b, s: (b, 0, s)
    s_major = pl.BlockSpec((1, tm, ATTN_WIDTH), row_blk)
    f_major = pl.BlockSpec((1, ATTN_WIDTH, tm), col_blk)
    vt_blk = pl.BlockSpec((1, HEAD_BLOCKS, tm // tk, LANES, tk), lambda b, s: (b, 0, s, 0, 0))
    s_shape = jax.ShapeDtypeStruct((B, S, ATTN_WIDTH), BF16)
    f_shape = jax.ShapeDtypeStruct((B, ATTN_WIDTH, S), BF16)
    vt_shape = jax.ShapeDtypeStruct((B, HEAD_BLOCKS, S // tk, LANES, tk), BF16)
    return pl.pallas_call(
        functools.partial(_attn_proj_kernel, tm=tm, tk=tk),
        grid=(B, S // tm),
        in_specs=[pl.BlockSpec((1, tm, D), row_blk)] + [pl.BlockSpec(c.shape, const) for c in consts],
        out_specs=[f_major, f_major, s_major, s_major, vt_blk, s_major],
        out_shape=[f_shape, f_shape, s_shape, s_shape, vt_shape, s_shape],
        scratch_shapes=[pltpu.VMEM((SUBLANES, LANES), F32)],
        compiler_params=pltpu.CompilerParams(
            dimension_semantics=("arbitrary", "arbitrary"), vmem_limit_bytes=VMEM_LIMIT),
        name="attn_proj",
    )(x, *consts)


def _flash_kernel(qt_ref, qat_ref, k_ref, ka_ref, vt_ref, o_ref, s_scr, p_scr, acc_scr, *, tq, tk):
    i = pl.program_id(2)
    n_visits = 2 * i + 2
    qrow = lax.broadcasted_iota(jnp.int32, (2 * LANES, tq), 0)
    vrow = lax.broadcasted_iota(jnp.int32, (LANES, tk), 0)
    key = lax.broadcasted_iota(jnp.int32, (tk, tq), 0)
    qry = lax.broadcasted_iota(jnp.int32, (tk, tq), 1)

    q_all = jnp.concatenate([qt_ref[0], qat_ref[0]], axis=0)
    q_heads, v_head_rows = [], []
    for j in range(HEADS_PER_BLOCK):
        lo = j * ATTN_HEAD_DIM
        slot = LANES + j * AUG_STRIDE
        mine = ((qrow >= lo) & (qrow < lo + ATTN_HEAD_DIM)) | ((qrow >= slot) & (qrow < slot + AUG_STRIDE))
        q_heads.append(jnp.where(mine, q_all, jnp.zeros_like(q_all)))
        v_head_rows.append((vrow >= lo) & (vrow < lo + ATTN_HEAD_DIM))
    ones = jnp.ones((LANES, tk), BF16)

    def block_of(visit):
        return jnp.where(visit == 0, 2 * i, jnp.where(visit == 1, 2 * i + 1, 2 * i + 1 - visit))

    def logits(visit, mask_offset=None):
        ks = pl.multiple_of(block_of(visit) * tk, tk)
        kk = jnp.concatenate([k_ref[0, pl.ds(ks, tk), :], ka_ref[0, pl.ds(ks, tk), :]], axis=1)
        maxima = []
        for j in range(HEADS_PER_BLOCK):
            s = jnp.dot(kk, q_heads[j], preferred_element_type=F32)
            if mask_offset is not None:
                s = jnp.where(key + mask_offset <= qry, s, -jnp.inf)
            s_scr[j] = s
            maxima.append(jnp.max(s, axis=0, keepdims=True))
        return tuple(maxima)

    def softmax(m, block_max):
        m_out, alpha_out = [], []
        for j in range(HEADS_PER_BLOCK):
            m_new = jnp.maximum(m[j], block_max[j])
            alpha_out.append(jnp.exp2(m[j] - m_new))
            p_scr[j] = jnp.exp2(s_scr[j] - m_new).astype(BF16)
            m_out.append(m_new)
        return tuple(m_out), tuple(alpha_out)

    def accumulate(visit, alpha):
        vt = vt_ref[0, 0, block_of(visit)]
        for j in range(HEADS_PER_BLOCK):
            v1 = jnp.where(v_head_rows[j], vt, ones)
            acc_scr[j] = alpha[j] * acc_scr[j] + jnp.dot(v1, p_scr[j], preferred_element_type=F32)

    acc_scr[...] = jnp.zeros_like(acc_scr)
    m = tuple(jnp.full((1, tq), -jnp.inf, F32) for _ in range(HEADS_PER_BLOCK))
    block_max = logits(0, mask_offset=0)
    m, alpha = softmax(m, block_max)
    block_max = logits(1, mask_offset=tk)

    def step(visit, carry):
        m, alpha_prev, block_max = carry
        accumulate(visit - 1, alpha_prev)
        m, alpha = softmax(m, block_max)
        return m, alpha, logits(visit + 1)

    m, alpha, block_max = lax.fori_loop(1, n_visits - 1, step, (m, alpha, block_max))
    accumulate(n_visits - 2, alpha)
    m, alpha = softmax(m, block_max)
    accumulate(n_visits - 1, alpha)

    half = ATTN_HEAD_DIM
    a0, a1 = acc_scr[0], acc_scr[1]
    o_t = jnp.concatenate([a0[:half] / a0[half:], a1[half:] / a1[:half]], axis=0)
    o_ref[0] = o_t.T.astype(BF16)


def _flash(qt, qat, k, ka, vt):
    B, S, _ = k.shape
    tq, tk = ATTN_TQ, ATTN_TK
    q_blk = pl.BlockSpec((1, LANES, tq), lambda b, h, i: (b, h, i))
    k_blk = pl.BlockSpec((1, S, LANES), lambda b, h, i: (b, 0, h))
    v_blk = pl.BlockSpec((1, 1, S // tk, LANES, tk), lambda b, h, i: (b, h, 0, 0, 0))
    return pl.pallas_call(
        functools.partial(_flash_kernel, tq=tq, tk=tk),
        grid=(B, HEAD_BLOCKS, S // tq),
        in_specs=[q_blk, q_blk, k_blk, k_blk, v_blk],
        out_specs=pl.BlockSpec((1, tq, LANES), lambda b, h, i: (b, i, h)),
        out_shape=jax.ShapeDtypeStruct((B, S, ATTN_WIDTH), BF16),
        scratch_shapes=[
            pltpu.VMEM((HEADS_PER_BLOCK, tk, tq), F32),
            pltpu.VMEM((HEADS_PER_BLOCK, tk, tq), BF16),
            pltpu.VMEM((HEADS_PER_BLOCK, LANES, tq), F32),
        ],
        compiler_params=pltpu.CompilerParams(
            dimension_semantics=("arbitrary", "arbitrary", "arbitrary"), vmem_limit_bytes=VMEM_LIMIT),
        name="fox_flash",
    )(qt, qat, k, ka, vt)


def _attn_out_kernel(o_ref, g_ref, x_ref, w_ref, lg_ref, lb_ref, y_ref):
    y = o_ref[0].astype(F32) * _silu(g_ref[0].astype(F32))
    h = jnp.dot(y.astype(BF16), w_ref[...], preferred_element_type=F32)
    y_ref[0] = _deepnorm_ln(x_ref[0], h, lg_ref[...], lb_ref[...])


def _attn_out(o, gate, x, w_out, ln_g, ln_b):
    B, S, D = x.shape
    tm = OUT_ROWS
    const = lambda b, s: (0, 0)
    row_blk = lambda b, s: (b, s, 0)
    return pl.pallas_call(
        _attn_out_kernel,
        grid=(B, S // tm),
        in_specs=[
            pl.BlockSpec((1, tm, ATTN_WIDTH), row_blk),
            pl.BlockSpec((1, tm, ATTN_WIDTH), row_blk),
            pl.BlockSpec((1, tm, D), row_blk),
            pl.BlockSpec(w_out.shape, const),
            pl.BlockSpec(ln_g.shape, const),
            pl.BlockSpec(ln_b.shape, const),
        ],
        out_specs=pl.BlockSpec((1, tm, D), row_blk),
        out_shape=jax.ShapeDtypeStruct((B, S, D), F32),
        compiler_params=pltpu.CompilerParams(
            dimension_semantics=("arbitrary", "arbitrary"), vmem_limit_bytes=VMEM_LIMIT),
        name="attn_out",
    )(o, gate, x, w_out, ln_g, ln_b)


def _rnn_kernel(x_ref, win_ref, cw_ref, cb_ref, wa_ref, ba_ref, wi_ref, bi_ref, lam_ref, wout_ref,
                lg_ref, lb_ref, y_ref, tail_ref, h_ref, *, tm):
    si = pl.program_id(1)

    @pl.when(si == 0)
    def _():
        tail_ref[...] = jnp.zeros_like(tail_ref)
        h_ref[...] = jnp.zeros_like(h_ref)

    w = RNN_WIDTH
    x = x_ref[0]
    xb = x.astype(BF16)
    u = jnp.dot(xb, win_ref[:, 0:w], preferred_element_type=F32)
    gate = jnp.dot(xb, win_ref[:, w:2 * w], preferred_element_type=F32)

    row8 = lax.broadcasted_iota(jnp.int32, (SUBLANES, w), 0)
    tail = tail_ref[...]
    conv = cb_ref[...] + u * cw_ref[CONV_WIDTH - 1:CONV_WIDTH, :]
    for back in range(1, CONV_WIDTH):
        shifted = pltpu.roll(u, back, 0)
        head = jnp.where(row8 < back, pltpu.roll(tail, back, 0), shifted[0:SUBLANES])
        shifted = jnp.concatenate([head, shifted[SUBLANES:]], axis=0)
        tap = CONV_WIDTH - 1 - back
        conv = conv + shifted * cw_ref[tap:tap + 1, :]
    tail_ref[...] = u[tm - SUBLANES:tm]

    cbf = conv.astype(BF16)
    bw = RNN_BLOCK_WIDTH
    ra = jnp.concatenate(
        [jnp.dot(cbf[:, n * bw:(n + 1) * bw], wa_ref[n], preferred_element_type=F32) for n in range(RNN_BLOCKS)],
        axis=1)
    ri = jnp.concatenate(
        [jnp.dot(cbf[:, n * bw:(n + 1) * bw], wi_ref[n], preferred_element_type=F32) for n in range(RNN_BLOCKS)],
        axis=1)
    r = _sigmoid(ra + ba_ref[...])
    i = _sigmoid(ri + bi_ref[...])
    log_a = (-LRU_C * _softplus(-lam_ref[...])) * r
    a = jnp.exp(log_a)
    b = jnp.sqrt(jnp.tanh(-log_a) * (a * a + 1.0)) * (i * conv)

    rowm = lax.broadcasted_iota(jnp.int32, (tm, w), 0) % SUBLANES
    for shift in (1, 2, 4):
        ap = pltpu.roll(a, shift, 0)
        bp = pltpu.roll(b, shift, 0)
        take = rowm >= shift
        b = jnp.where(take, a * bp + b, b)
        a = jnp.where(take, a * ap, a)
    carry = h_ref[0:1, :]
    hs = []
    for g in range(tm // SUBLANES):
        lo = g * SUBLANES
        hg = a[lo:lo + SUBLANES] * carry + b[lo:lo + SUBLANES]
        carry = hg[SUBLANES - 1:SUBLANES]
        hs.append(hg)
    h_ref[0:1, :] = carry
    h = jnp.concatenate(hs, axis=0)

    y = h * _silu(gate)
    out = jnp.dot(y.astype(BF16), wout_ref[...], preferred_element_type=F32)
    y_ref[0] = _deepnorm_ln(x, out, lg_ref[...], lb_ref[...])


def _rnn_layer(x, w_in, conv_w, conv_b, w_a, b_a, w_i, b_i, lam, w_out, ln_g, ln_b):
    B, S, D = x.shape
    tm = RNN_ROWS
    row_blk = lambda b, s: (b, s, 0)

    def const(arr):
        nd = arr.ndim
        return pl.BlockSpec(arr.shape, lambda b, s: (0,) * nd)

    consts = [w_in, conv_w, conv_b, w_a, b_a, w_i, b_i, lam, w_out, ln_g, ln_b]
    return pl.pallas_call(
        functools.partial(_rnn_kernel, tm=tm),
        grid=(B, S // tm),
        in_specs=[pl.BlockSpec((1, tm, D), row_blk)] + [const(c) for c in consts],
        out_specs=pl.BlockSpec((1, tm, D), row_blk),
        out_shape=jax.ShapeDtypeStruct((B, S, D), F32),
        scratch_shapes=[pltpu.VMEM((SUBLANES, RNN_WIDTH), F32), pltpu.VMEM((SUBLANES, RNN_WIDTH), F32)],
        compiler_params=pltpu.CompilerParams(
            dimension_semantics=("arbitrary", "arbitrary"), vmem_limit_bytes=VMEM_LIMIT),
        name="rglru_layer",
    )(x, *consts)


def _row(v):
    return v.reshape(1, -1).astype(F32)


def kernel(x, ln_g, ln_b, attn_w_in, attn_b_f, attn_w_out, rnn_w_in, rnn_conv_w, rnn_conv_b,
           rnn_w_a, rnn_b_a, rnn_w_i, rnn_b_i, rnn_lambda, rnn_w_out):
    assert x.shape[-1] == D_MODEL and x.shape[1] % ATTN_TQ == 0
    w = ATTN_WIDTH
    for layer in range(DEPTH):
        idx = layer // N_MIXERS
        g, b = _row(ln_g[layer]), _row(ln_b[layer])
        if layer % N_MIXERS == 0:
            w_in = attn_w_in[idx].astype(BF16)
            w_f = jnp.pad(w_in[:, 4 * w:], ((0, 0), (0, LANES - ATTN_HEADS)))
            b_f = jnp.pad(attn_b_f[idx].astype(F32), (0, LANES - ATTN_HEADS)).reshape(1, LANES)
            qt, qat, k, ka, vt, gate = _attn_proj(
                x, w_in[:, w:2 * w], w_in[:, 3 * w:4 * w], w_in[:, 0:w].T, w_in[:, 2 * w:3 * w].T, w_f, b_f)
            o = _flash(qt, qat, k, ka, vt)
            x = _attn_out(o, gate, x, attn_w_out[idx].astype(BF16), g, b)
        else:
            x = _rnn_layer(
                x, rnn_w_in[idx].astype(BF16), rnn_conv_w[idx].astype(F32), _row(rnn_conv_b[idx]),
                rnn_w_a[idx].astype(BF16), _row(rnn_b_a[idx]), rnn_w_i[idx].astype(BF16), _row(rnn_b_i[idx]),
                _row(rnn_lambda[idx]), rnn_w_out[idx].astype(BF16), g, b)
    return x
```

```python
import functools

import jax
import jax.numpy as jnp
import numpy as np
from jax import lax
from jax.experimental import pallas as pl
from jax.experimental.pallas import tpu as pltpu

F32 = jnp.float32
BF16 = jnp.bfloat16

D_MODEL = 1024
DEPTH = 4
N_MIXERS = 2
ATTN_HEADS = 16
ATTN_HEAD_DIM = 64
ATTN_WIDTH = ATTN_HEADS * ATTN_HEAD_DIM
RNN_WIDTH = D_MODEL
RNN_BLOCK_WIDTH = 256
RNN_BLOCKS = RNN_WIDTH // RNN_BLOCK_WIDTH
CONV_WIDTH = 4
LRU_C = 8.0
DEEPNORM_ALPHA = (2.0 * DEPTH) ** 0.25
LN_EPS = 1e-5
LOG2_E = 1.4426950408889634

LANES = 128
SUBLANES = 8
HEADS_PER_BLOCK = LANES // ATTN_HEAD_DIM
HEAD_BLOCKS = ATTN_HEADS // HEADS_PER_BLOCK
AUG_STRIDE = 8
CUM_PARTS = 3
ONES_LANE = ATTN_HEADS

VMEM_LIMIT = 56 * 1024 * 1024

PROJ_ROWS = 512
OUT_ROWS = 512
RNN_ROWS = 256
ATTN_TK = 256
ATTN_TQ = 2 * ATTN_TK
ATTN_HEAD_BLOCKS_PER_STEP = 2
BF16_SUBLANES = 2 * SUBLANES
ACC_ROWS = ATTN_HEAD_DIM + BF16_SUBLANES


def _softplus(x):
    return jnp.maximum(x, 0.0) + jnp.log1p(jnp.exp(-jnp.abs(x)))


def _log_sigmoid(x):
    return jnp.minimum(x, 0.0) - jnp.log1p(jnp.exp(-jnp.abs(x)))


def _sigmoid(x):
    return 0.5 * jnp.tanh(0.5 * x) + 0.5


def _silu(x):
    return x * _sigmoid(x)


def _split_bf16(x):
    hi = x.astype(BF16)
    r1 = x - hi.astype(F32)
    mid = r1.astype(BF16)
    lo = (r1 - mid.astype(F32)).astype(BF16)
    return hi, mid, lo


def _deepnorm_ln(x, h, g, b):
    z = DEEPNORM_ALPHA * x + h
    mu = jnp.mean(z, axis=-1, keepdims=True)
    zc = z - mu
    var = jnp.mean(zc * zc, axis=-1, keepdims=True)
    return zc * lax.rsqrt(var + LN_EPS) * g + b


def _dot_nt(a, b):
    return lax.dot_general(a, b, (((1,), (1,)), ((), ())), preferred_element_type=F32)


def _attn_proj_kernel(x_ref, wk_ref, wg_ref, wqt_ref, wvt_ref, wf_ref, bf_ref, tri_ref, eqt_ref, ek_ref,
                      qt_ref, qat_ref, k_ref, ka_ref, vt_ref, g_ref, carry_ref, *, tm, tk):
    si = pl.program_id(1)

    @pl.when(si == 0)
    def _():
        carry_ref[...] = jnp.zeros_like(carry_ref)

    xb = x_ref[0].astype(BF16)
    k_ref[0] = jnp.dot(xb, wk_ref[...], preferred_element_type=F32).astype(BF16)
    g_ref[0] = jnp.dot(xb, wg_ref[...], preferred_element_type=F32).astype(BF16)
    qt_ref[0] = (_dot_nt(wqt_ref[...], xb) * (ATTN_HEAD_DIM ** -0.5 * LOG2_E)).astype(BF16)
    vt = _dot_nt(wvt_ref[...], xb).astype(BF16)
    for hb in range(HEAD_BLOCKS):
        for c in range(tm // tk):
            vt_ref[0, hb, c] = vt[hb * LANES:(hb + 1) * LANES, c * tk:(c + 1) * tk]

    f = jnp.dot(xb, wf_ref[...], preferred_element_type=F32) + bf_ref[...]
    ls = _log_sigmoid(f)
    tri = tri_ref[...]
    cum = carry_ref[0:1, :]
    for part in _split_bf16(ls):
        cum = cum + jnp.dot(tri, part, preferred_element_type=F32)
    carry_ref[0:1, :] = cum[tm - 1:tm, :]

    hi, mid, lo = _split_bf16(cum * LOG2_E)
    lane = lax.broadcasted_iota(jnp.int32, hi.shape, 1)
    hi = jnp.where(lane == ONES_LANE, jnp.ones_like(hi), hi)
    parts = jnp.concatenate([hi, mid, lo], axis=1)
    qat_ref[0] = _dot_nt(eqt_ref[...], parts).astype(BF16)
    ka_ref[0] = jnp.dot(parts, ek_ref[...], preferred_element_type=F32).astype(BF16)


def _bias_expanders():
    assert ATTN_HEADS * AUG_STRIDE == LANES
    eq = np.zeros((CUM_PARTS * LANES, LANES), np.float32)
    ek = np.zeros_like(eq)
    for h in range(ATTN_HEADS):
        base = h * AUG_STRIDE
        for p in range(CUM_PARTS):
            eq[p * LANES + h, base + p] = 1.0
            ek[ONES_LANE, base + p] = 1.0
            ek[p * LANES + h, base + CUM_PARTS + p] = -1.0
            eq[ONES_LANE, base + CUM_PARTS + p] = 1.0
    return jnp.asarray(eq.T, BF16), jnp.asarray(ek, BF16)


def _attn_proj(x, w_k, w_g, w_qt, w_vt, w_f, b_f):
    B, S, D = x.shape
    tm, tk = PROJ_ROWS, ATTN_TK
    tri = jnp.asarray(np.tril(np.ones((tm, tm), np.float32)), BF16)
    eqt, ek = _bias_expanders()
    consts = [w_k, w_g, w_qt, w_vt, w_f, b_f, tri, eqt, ek]
    const = lambda b, s: (0, 0)
    row_blk = lambda b, s: (b, s, 0)
    col_blk = lambda b, s: (b, 0, s)
    s_major = pl.BlockSpec((1, tm, ATTN_WIDTH), row_blk)
    f_major = pl.BlockSpec((1, ATTN_WIDTH, tm), col_blk)
    vt_blk = pl.BlockSpec((1, HEAD_BLOCKS, tm // tk, LANES, tk), lambda b, s: (b, 0, s, 0, 0))
    s_shape = jax.ShapeDtypeStruct((B, S, ATTN_WIDTH), BF16)
    f_shape = jax.ShapeDtypeStruct((B, ATTN_WIDTH, S), BF16)
    vt_shape = jax.ShapeDtypeStruct((B, HEAD_BLOCKS, S // tk, LANES, tk), BF16)
    s_slots = pl.BlockSpec((1, tm, LANES), row_blk)
    f_slots = pl.BlockSpec((1, LANES, tm), col_blk)
    s_slots_shape = jax.ShapeDtypeStruct((B, S, LANES), BF16)
    f_slots_shape = jax.ShapeDtypeStruct((B, LANES, S), BF16)
    return pl.pallas_call(
        functools.partial(_attn_proj_kernel, tm=tm, tk=tk),
        grid=(B, S // tm),
        in_specs=[pl.BlockSpec((1, tm, D), row_blk)] + [pl.BlockSpec(c.shape, const) for c in consts],
        out_specs=[f_major, f_slots, s_major, s_slots, vt_blk, s_major],
        out_shape=[f_shape, f_slots_shape, s_shape, s_slots_shape, vt_shape, s_shape],
        scratch_shapes=[pltpu.VMEM((SUBLANES, LANES), F32)],
        compiler_params=pltpu.CompilerParams(
            dimension_semantics=("arbitrary", "arbitrary"), vmem_limit_bytes=VMEM_LIMIT),
        name="attn_proj",
    )(x, *consts)


def _flash_kernel(qt_ref, qat_ref, k_ref, ka_ref, vt_ref, o_ref, s_scr, p_scr, acc_scr, *, tq, tk, hb):
    i = pl.program_id(2)
    n_visits = 2 * i + 2
    n_heads = hb * HEADS_PER_BLOCK
    qrow = lax.broadcasted_iota(jnp.int32, (2 * LANES, tq), 0)
    key = lax.broadcasted_iota(jnp.int32, (tk, tq), 0)
    qry = lax.broadcasted_iota(jnp.int32, (tk, tq), 1)
    ones = jnp.ones((ACC_ROWS - ATTN_HEAD_DIM, tk), BF16)

    first_head = pl.program_id(1) * n_heads
    ka_all = ka_ref.at[0]
    q_heads = []
    for c in range(hb):
        q_all = jnp.concatenate([qt_ref[0, c * LANES:(c + 1) * LANES], qat_ref[0]], axis=0)
        for j in range(HEADS_PER_BLOCK):
            lo = j * ATTN_HEAD_DIM
            slot = LANES + (first_head + c * HEADS_PER_BLOCK + j) * AUG_STRIDE
            mine = ((qrow >= lo) & (qrow < lo + ATTN_HEAD_DIM)) | ((qrow >= slot) & (qrow < slot + AUG_STRIDE))
            q_heads.append(jnp.where(mine, q_all, jnp.zeros_like(q_all)))

    def block_of(visit):
        return jnp.where(visit == 0, 2 * i, jnp.where(visit == 1, 2 * i + 1, 2 * i + 1 - visit))

    def logits(visit, mask_offset=None):
        ks = pl.multiple_of(block_of(visit) * tk, tk)
        maxima = []
        ka = ka_all[pl.ds(ks, tk), :]
        for c in range(hb):
            kk = jnp.concatenate([k_ref[0, pl.ds(ks, tk), c * LANES:(c + 1) * LANES], ka], axis=1)
            for j in range(HEADS_PER_BLOCK):
                h = c * HEADS_PER_BLOCK + j
                s = jnp.dot(kk, q_heads[h], preferred_element_type=F32)
                if mask_offset is not None:
                    s = jnp.where(key + mask_offset <= qry, s, -jnp.inf)
                s_scr[h] = s
                maxima.append(jnp.max(s, axis=0, keepdims=True))
        return tuple(maxima)

    def softmax(m, block_max):
        m_out, alpha_out = [], []
        for h in range(n_heads):
            m_new = jnp.maximum(m[h], block_max[h])
            alpha_out.append(jnp.exp2(m[h] - m_new))
            p_scr[h] = jnp.exp2(s_scr[h] - m_new).astype(BF16)
            m_out.append(m_new)
        return tuple(m_out), tuple(alpha_out)

    def accumulate(visit, alpha):
        kb = block_of(visit)
        for c in range(hb):
            vt = vt_ref[0, c, kb]
            for j in range(HEADS_PER_BLOCK):
                h = c * HEADS_PER_BLOCK + j
                v1 = jnp.concatenate([vt[j * ATTN_HEAD_DIM:(j + 1) * ATTN_HEAD_DIM], ones], axis=0)
                acc_scr[h] = alpha[h] * acc_scr[h] + jnp.dot(v1, p_scr[h], preferred_element_type=F32)

    acc_scr[...] = jnp.zeros_like(acc_scr)
    m = tuple(jnp.full((1, tq), -jnp.inf, F32) for _ in range(n_heads))
    block_max = logits(0, mask_offset=0)
    m, alpha = softmax(m, block_max)
    block_max = logits(1, mask_offset=tk)

    def step(visit, carry):
        m, alpha_prev, block_max = carry
        accumulate(visit - 1, alpha_prev)
        m, alpha = softmax(m, block_max)
        return m, alpha, logits(visit + 1)

    def step_pair(t, carry):
        return step(2 * t + 2, step(2 * t + 1, carry))

    m, alpha, block_max = lax.fori_loop(0, i, step_pair, (m, alpha, block_max))
    accumulate(n_visits - 2, alpha)
    m, alpha = softmax(m, block_max)
    accumulate(n_visits - 1, alpha)

    d = ATTN_HEAD_DIM
    for c in range(hb):
        pair = []
        for j in range(HEADS_PER_BLOCK):
            acc = acc_scr[c * HEADS_PER_BLOCK + j]
            pair.append(acc[:d] / acc[d:d + 1])
        o_ref[0, :, c * LANES:(c + 1) * LANES] = jnp.concatenate(pair, axis=0).T.astype(BF16)


def _flash(qt, qat, k, ka, vt):
    B, S, _ = k.shape
    tq, tk, hb = ATTN_TQ, ATTN_TK, ATTN_HEAD_BLOCKS_PER_STEP
    n_heads = hb * HEADS_PER_BLOCK
    q_blk = pl.BlockSpec((1, hb * LANES, tq), lambda b, h, i: (b, h, i))
    q_slots = pl.BlockSpec((1, LANES, tq), lambda b, h, i: (b, 0, i))
    k_blk = pl.BlockSpec((1, S, hb * LANES), lambda b, h, i: (b, 0, h))
    k_slots = pl.BlockSpec((1, S, LANES), lambda b, h, i: (b, 0, 0))
    v_blk = pl.BlockSpec((1, hb, S // tk, LANES, tk), lambda b, h, i: (b, h, 0, 0, 0))
    return pl.pallas_call(
        functools.partial(_flash_kernel, tq=tq, tk=tk, hb=hb),
        grid=(B, HEAD_BLOCKS // hb, S // tq),
        in_specs=[q_blk, q_slots, k_blk, k_slots, v_blk],
        out_specs=pl.BlockSpec((1, tq, hb * LANES), lambda b, h, i: (b, i, h)),
        out_shape=jax.ShapeDtypeStruct((B, S, ATTN_WIDTH), BF16),
        scratch_shapes=[
            pltpu.VMEM((n_heads, tk, tq), F32),
            pltpu.VMEM((n_heads, tk, tq), BF16),
            pltpu.VMEM((n_heads, ACC_ROWS, tq), F32),
        ],
        compiler_params=pltpu.CompilerParams(
            dimension_semantics=("arbitrary", "arbitrary", "arbitrary"), vmem_limit_bytes=VMEM_LIMIT),
        name="fox_flash",
    )(qt, qat, k, ka, vt)


def _attn_out_kernel(o_ref, g_ref, x_ref, w_ref, lg_ref, lb_ref, y_ref):
    y = o_ref[0].astype(F32) * _silu(g_ref[0].astype(F32))
    h = jnp.dot(y.astype(BF16), w_ref[...], preferred_element_type=F32)
    y_ref[0] = _deepnorm_ln(x_ref[0], h, lg_ref[...], lb_ref[...])


def _attn_out(o, gate, x, w_out, ln_g, ln_b):
    B, S, D = x.shape
    tm = OUT_ROWS
    const = lambda b, s: (0, 0)
    row_blk = lambda b, s: (b, s, 0)
    return pl.pallas_call(
        _attn_out_kernel,
        grid=(B, S // tm),
        in_specs=[
            pl.BlockSpec((1, tm, ATTN_WIDTH), row_blk),
            pl.BlockSpec((1, tm, ATTN_WIDTH), row_blk),
            pl.BlockSpec((1, tm, D), row_blk),
            pl.BlockSpec(w_out.shape, const),
            pl.BlockSpec(ln_g.shape, const),
            pl.BlockSpec(ln_b.shape, const),
        ],
        out_specs=pl.BlockSpec((1, tm, D), row_blk),
        out_shape=jax.ShapeDtypeStruct((B, S, D), F32),
        compiler_params=pltpu.CompilerParams(
            dimension_semantics=("arbitrary", "arbitrary"), vmem_limit_bytes=VMEM_LIMIT),
        name="attn_out",
    )(o, gate, x, w_out, ln_g, ln_b)


def _rnn_kernel(x_ref, win_ref, cw_ref, cb_ref, wa_ref, ba_ref, wi_ref, bi_ref, lam_ref, wout_ref,
                lg_ref, lb_ref, y_ref, tail_ref, h_ref, *, tm):
    si = pl.program_id(1)

    @pl.when(si == 0)
    def _():
        tail_ref[...] = jnp.zeros_like(tail_ref)
        h_ref[...] = jnp.zeros_like(h_ref)

    w = RNN_WIDTH
    x = x_ref[0]
    xb = x.astype(BF16)
    u = jnp.dot(xb, win_ref[:, 0:w], preferred_element_type=F32)
    gate = jnp.dot(xb, win_ref[:, w:2 * w], preferred_element_type=F32)

    row8 = lax.broadcasted_iota(jnp.int32, (SUBLANES, w), 0)
    tail = tail_ref[...]
    conv = cb_ref[...] + u * cw_ref[CONV_WIDTH - 1:CONV_WIDTH, :]
    for back in range(1, CONV_WIDTH):
        shifted = pltpu.roll(u, back, 0)
        head = jnp.where(row8 < back, pltpu.roll(tail, back, 0), shifted[0:SUBLANES])
        shifted = jnp.concatenate([head, shifted[SUBLANES:]], axis=0)
        tap = CONV_WIDTH - 1 - back
        conv = conv + shifted * cw_ref[tap:tap + 1, :]
    tail_ref[...] = u[tm - SUBLANES:tm]

    cbf = conv.astype(BF16)
    bw = RNN_BLOCK_WIDTH
    ra = jnp.concatenate(
        [jnp.dot(cbf[:, n * bw:(n + 1) * bw], wa_ref[n], preferred_element_type=F32) for n in range(RNN_BLOCKS)],
        axis=1)
    ri = jnp.concatenate(
        [jnp.dot(cbf[:, n * bw:(n + 1) * bw], wi_ref[n], preferred_element_type=F32) for n in range(RNN_BLOCKS)],
        axis=1)
    r = _sigmoid(ra + ba_ref[...])
    i = _sigmoid(ri + bi_ref[...])
    log_a = (-LRU_C * _softplus(-lam_ref[...])) * r
    a = jnp.exp(log_a)
    b = jnp.sqrt(jnp.tanh(-log_a) * (a * a + 1.0)) * (i * conv)

    rowm = lax.broadcasted_iota(jnp.int32, (tm, w), 0) % SUBLANES
    for shift in (1, 2, 4):
        ap = pltpu.roll(a, shift, 0)
        bp = pltpu.roll(b, shift, 0)
        take = rowm >= shift
        b = jnp.where(take, a * bp + b, b)
        a = jnp.where(take, a * ap, a)
    carry = h_ref[0:1, :]
    hs = []
    for g in range(tm // SUBLANES):
        lo = g * SUBLANES
        hg = a[lo:lo + SUBLANES] * carry + b[lo:lo + SUBLANES]
        carry = hg[SUBLANES - 1:SUBLANES]
        hs.append(hg)
    h_ref[0:1, :] = carry
    h = jnp.concatenate(hs, axis=0)

    y = h * _silu(gate)
    out = jnp.dot(y.astype(BF16), wout_ref[...], preferred_element_type=F32)
    y_ref[0] = _deepnorm_ln(x, out, lg_ref[...], lb_ref[...])


def _rnn_layer(x, w_in, conv_w, conv_b, w_a, b_a, w_i, b_i, lam, w_out, ln_g, ln_b):
    B, S, D = x.shape
    tm = RNN_ROWS
    row_blk = lambda b, s: (b, s, 0)

    def const(arr):
        nd = arr.ndim
        return pl.BlockSpec(arr.shape, lambda b, s: (0,) * nd)

    consts = [w_in, conv_w, conv_b, w_a, b_a, w_i, b_i, lam, w_out, ln_g, ln_b]
    return pl.pallas_call(
        functools.partial(_rnn_kernel, tm=tm),
        grid=(B, S // tm),
        in_specs=[pl.BlockSpec((1, tm, D), row_blk)] + [const(c) for c in consts],
        out_specs=pl.BlockSpec((1, tm, D), row_blk),
        out_shape=jax.ShapeDtypeStruct((B, S, D), F32),
        scratch_shapes=[pltpu.VMEM((SUBLANES, RNN_WIDTH), F32), pltpu.VMEM((SUBLANES, RNN_WIDTH), F32)],
        compiler_params=pltpu.CompilerParams(
            dimension_semantics=("arbitrary", "arbitrary"), vmem_limit_bytes=VMEM_LIMIT),
        name="rglru_layer",
    )(x, *consts)


def _row(v):
    return v.reshape(1, -1).astype(F32)


def kernel(x, ln_g, ln_b, attn_w_in, attn_b_f, attn_w_out, rnn_w_in, rnn_conv_w, rnn_conv_b,
           rnn_w_a, rnn_b_a, rnn_w_i, rnn_b_i, rnn_lambda, rnn_w_out):
    assert x.shape[-1] == D_MODEL and x.shape[1] % ATTN_TQ == 0
    w = ATTN_WIDTH
    for layer in range(DEPTH):
        idx = layer // N_MIXERS
        g, b = _row(ln_g[layer]), _row(ln_b[layer])
        if layer % N_MIXERS == 0:
            w_in = attn_w_in[idx].astype(BF16)
            w_f = jnp.pad(w_in[:, 4 * w:], ((0, 0), (0, LANES - ATTN_HEADS)))
            b_f = jnp.pad(attn_b_f[idx].astype(F32), (0, LANES - ATTN_HEADS)).reshape(1, LANES)
            qt, qat, k, ka, vt, gate = _attn_proj(
                x, w_in[:, w:2 * w], w_in[:, 3 * w:4 * w], w_in[:, 0:w].T, w_in[:, 2 * w:3 * w].T, w_f, b_f)
            o = _flash(qt, qat, k, ka, vt)
            x = _attn_out(o, gate, x, attn_w_out[idx].astype(BF16), g, b)
        else:
            x = _rnn_layer(
                x, rnn_w_in[idx].astype(BF16), rnn_conv_w[idx].astype(F32), _row(rnn_conv_b[idx]),
                rnn_w_a[idx].astype(BF16), _row(rnn_b_a[idx]), rnn_w_i[idx].astype(BF16), _row(rnn_b_i[idx]),
                _row(rnn_lambda[idx]), rnn_w_out[idx].astype(BF16), g, b)
    return x
```

```python
import functools

import jax
import jax.numpy as jnp
import numpy as np
from jax import lax
from jax.experimental import pallas as pl
from jax.experimental.pallas import tpu as pltpu

F32 = jnp.float32
BF16 = jnp.bfloat16

D_MODEL = 1024
DEPTH = 4
N_MIXERS = 2
ATTN_HEADS = 16
ATTN_HEAD_DIM = 64
ATTN_WIDTH = ATTN_HEADS * ATTN_HEAD_DIM
RNN_WIDTH = D_MODEL
RNN_BLOCK_WIDTH = 256
RNN_BLOCKS = RNN_WIDTH // RNN_BLOCK_WIDTH
CONV_WIDTH = 4
LRU_C = 8.0
DEEPNORM_ALPHA = (2.0 * DEPTH) ** 0.25
LN_EPS = 1e-5
LOG2_E = 1.4426950408889634
F32_MIN_NORMAL = float(np.finfo(np.float32).tiny)

LANES = 128
SUBLANES = 8
HEADS_PER_BLOCK = LANES // ATTN_HEAD_DIM
HEAD_BLOCKS = ATTN_HEADS // HEADS_PER_BLOCK
AUG_STRIDE = 8
CUM_PARTS = 3
ONES_LANE = ATTN_HEADS

VMEM_LIMIT = 56 * 1024 * 1024

PROJ_ROWS = 512
OUT_ROWS = 512
RNN_ROWS = 256
RNN_BATCH_ROWS = 2
RNN_STAGES = 5
RNN_STAGE_SKEW = 2
ATTN_TK = 256
ATTN_TQ = 2 * ATTN_TK
ATTN_HEAD_BLOCKS_PER_STEP = 2
BF16_SUBLANES = 2 * SUBLANES
ACC_ROWS = ATTN_HEAD_DIM + BF16_SUBLANES


def _softplus(x):
    return jnp.maximum(x, 0.0) + jnp.log1p(jnp.exp(-jnp.abs(x)))


def _log_sigmoid(x):
    return jnp.minimum(x, 0.0) - jnp.log1p(jnp.exp(-jnp.abs(x)))


def _sigmoid(x):
    return 0.5 * jnp.tanh(0.5 * x) + 0.5


def _silu(x):
    return x * _sigmoid(x)


def _split_bf16(x):
    hi = x.astype(BF16)
    r1 = x - hi.astype(F32)
    mid = r1.astype(BF16)
    lo = (r1 - mid.astype(F32)).astype(BF16)
    return hi, mid, lo


def _deepnorm_ln(x, h, g, b):
    z = DEEPNORM_ALPHA * x + h
    mu = jnp.mean(z, axis=-1, keepdims=True)
    zc = z - mu
    var = jnp.mean(zc * zc, axis=-1, keepdims=True)
    return zc * lax.rsqrt(var + LN_EPS) * g + b


def _dot_nt(a, b):
    return lax.dot_general(a, b, (((1,), (1,)), ((), ())), preferred_element_type=F32)


def _attn_proj_kernel(x_ref, wk_ref, wg_ref, wqt_ref, wvt_ref, wf_ref, bf_ref, tri_ref, eqt_ref, ek_ref,
                      qt_ref, qat_ref, k_ref, ka_ref, vt_ref, g_ref, carry_ref, *, tm, tk):
    si = pl.program_id(1)

    @pl.when(si == 0)
    def _():
        carry_ref[...] = jnp.zeros_like(carry_ref)

    xb = x_ref[0].astype(BF16)
    k_ref[0] = jnp.dot(xb, wk_ref[...], preferred_element_type=F32).astype(BF16)
    g_ref[0] = jnp.dot(xb, wg_ref[...], preferred_element_type=F32).astype(BF16)
    qt_ref[0] = (_dot_nt(wqt_ref[...], xb) * (ATTN_HEAD_DIM ** -0.5 * LOG2_E)).astype(BF16)
    vt = _dot_nt(wvt_ref[...], xb).astype(BF16)
    for hb in range(HEAD_BLOCKS):
        for c in range(tm // tk):
            vt_ref[0, hb, c] = vt[hb * LANES:(hb + 1) * LANES, c * tk:(c + 1) * tk]

    f = jnp.dot(xb, wf_ref[...], preferred_element_type=F32) + bf_ref[...]
    ls = _log_sigmoid(f)
    tri = tri_ref[...]
    cum = carry_ref[0:1, :]
    for part in _split_bf16(ls):
        cum = cum + jnp.dot(tri, part, preferred_element_type=F32)
    carry_ref[0:1, :] = cum[tm - 1:tm, :]

    hi, mid, lo = _split_bf16(cum * LOG2_E)
    lane = lax.broadcasted_iota(jnp.int32, hi.shape, 1)
    hi = jnp.where(lane == ONES_LANE, jnp.ones_like(hi), hi)
    parts = jnp.concatenate([hi, mid, lo], axis=1)
    qat_ref[0] = _dot_nt(eqt_ref[...], parts).astype(BF16)
    ka_ref[0] = jnp.dot(parts, ek_ref[...], preferred_element_type=F32).astype(BF16)


def _bias_expanders():
    assert ATTN_HEADS * AUG_STRIDE == LANES
    eq = np.zeros((CUM_PARTS * LANES, LANES), np.float32)
    ek = np.zeros_like(eq)
    for h in range(ATTN_HEADS):
        base = h * AUG_STRIDE
        for p in range(CUM_PARTS):
            eq[p * LANES + h, base + p] = 1.0
            ek[ONES_LANE, base + p] = 1.0
            ek[p * LANES + h, base + CUM_PARTS + p] = -1.0
            eq[ONES_LANE, base + CUM_PARTS + p] = 1.0
    return jnp.asarray(eq.T, BF16), jnp.asarray(ek, BF16)


def _attn_proj(x, w_k, w_g, w_qt, w_vt, w_f, b_f):
    B, S, D = x.shape
    tm, tk = PROJ_ROWS, ATTN_TK
    tri = jnp.asarray(np.tril(np.ones((tm, tm), np.float32)), BF16)
    eqt, ek = _bias_expanders()
    consts = [w_k, w_g, w_qt, w_vt, w_f, b_f, tri, eqt, ek]
    const = lambda b, s: (0, 0)
    row_blk = lambda b, s: (b, s, 0)
    col_blk = lambda b, s: (b, 0, s)
    s_major = pl.BlockSpec((1, tm, ATTN_WIDTH), row_blk)
    f_major = pl.BlockSpec((1, ATTN_WIDTH, tm), col_blk)
    vt_blk = pl.BlockSpec((1, HEAD_BLOCKS, tm // tk, LANES, tk), lambda b, s: (b, 0, s, 0, 0))
    s_shape = jax.ShapeDtypeStruct((B, S, ATTN_WIDTH), BF16)
    f_shape = jax.ShapeDtypeStruct((B, ATTN_WIDTH, S), BF16)
    vt_shape = jax.ShapeDtypeStruct((B, HEAD_BLOCKS, S // tk, LANES, tk), BF16)
    s_slots = pl.BlockSpec((1, tm, LANES), row_blk)
    f_slots = pl.BlockSpec((1, LANES, tm), col_blk)
    s_slots_shape = jax.ShapeDtypeStruct((B, S, LANES), BF16)
    f_slots_shape = jax.ShapeDtypeStruct((B, LANES, S), BF16)
    return pl.pallas_call(
        functools.partial(_attn_proj_kernel, tm=tm, tk=tk),
        grid=(B, S // tm),
        in_specs=[pl.BlockSpec((1, tm, D), row_blk)] + [pl.BlockSpec(c.shape, const) for c in consts],
        out_specs=[f_major, f_slots, s_major, s_slots, vt_blk, s_major],
        out_shape=[f_shape, f_slots_shape, s_shape, s_slots_shape, vt_shape, s_shape],
        scratch_shapes=[pltpu.VMEM((SUBLANES, LANES), F32)],
        compiler_params=pltpu.CompilerParams(
            dimension_semantics=("arbitrary", "arbitrary"), vmem_limit_bytes=VMEM_LIMIT),
        name="attn_proj",
    )(x, *consts)


def _flash_kernel(qt_ref, qat_ref, k_ref, ka_ref, vt_ref, o_ref, s_scr, p_scr, acc_scr, *, tq, tk, hb):
    i = pl.program_id(2)
    n_visits = 2 * i + 2
    n_heads = hb * HEADS_PER_BLOCK
    qrow = lax.broadcasted_iota(jnp.int32, (2 * LANES, tq), 0)
    key = lax.broadcasted_iota(jnp.int32, (tk, tq), 0)
    qry = lax.broadcasted_iota(jnp.int32, (tk, tq), 1)
    ones = jnp.ones((ACC_ROWS - ATTN_HEAD_DIM, tk), BF16)

    first_head = pl.program_id(1) * n_heads
    ka_all = ka_ref.at[0]
    q_heads = []
    for c in range(hb):
        q_all = jnp.concatenate([qt_ref[0, c * LANES:(c + 1) * LANES], qat_ref[0]], axis=0)
        for j in range(HEADS_PER_BLOCK):
            lo = j * ATTN_HEAD_DIM
            slot = LANES + (first_head + c * HEADS_PER_BLOCK + j) * AUG_STRIDE
            mine = ((qrow >= lo) & (qrow < lo + ATTN_HEAD_DIM)) | ((qrow >= slot) & (qrow < slot + AUG_STRIDE))
            q_heads.append(jnp.where(mine, q_all, jnp.zeros_like(q_all)))

    def block_of(visit):
        return jnp.where(visit == 0, 2 * i, jnp.where(visit == 1, 2 * i + 1, 2 * i + 1 - visit))

    def logits(visit, mask_offset=None):
        ks = pl.multiple_of(block_of(visit) * tk, tk)
        maxima = []
        ka = ka_all[pl.ds(ks, tk), :]
        for c in range(hb):
            kk = jnp.concatenate([k_ref[0, pl.ds(ks, tk), c * LANES:(c + 1) * LANES], ka], axis=1)
            for j in range(HEADS_PER_BLOCK):
                h = c * HEADS_PER_BLOCK + j
                s = jnp.dot(kk, q_heads[h], preferred_element_type=F32)
                if mask_offset is not None:
                    s = jnp.where(key + mask_offset <= qry, s, -jnp.inf)
                s_scr[h] = s
                maxima.append(jnp.max(s, axis=0, keepdims=True))
        return tuple(maxima)

    def softmax(m, block_max):
        m_out, alpha_out = [], []
        for h in range(n_heads):
            m_new = jnp.maximum(m[h], block_max[h])
            alpha_out.append(jnp.exp2(m[h] - m_new))
            p_scr[h] = jnp.exp2(s_scr[h] - m_new).astype(BF16)
            m_out.append(m_new)
        return tuple(m_out), tuple(alpha_out)

    def accumulate(visit, alpha):
        kb = block_of(visit)
        for c in range(hb):
            vt = vt_ref[0, c, kb]
            for j in range(HEADS_PER_BLOCK):
                h = c * HEADS_PER_BLOCK + j
                v1 = jnp.concatenate([vt[j * ATTN_HEAD_DIM:(j + 1) * ATTN_HEAD_DIM], ones], axis=0)
                acc_scr[h] = alpha[h] * acc_scr[h] + jnp.dot(v1, p_scr[h], preferred_element_type=F32)

    acc_scr[...] = jnp.zeros_like(acc_scr)
    m = tuple(jnp.full((1, tq), -jnp.inf, F32) for _ in range(n_heads))
    block_max = logits(0, mask_offset=0)
    m, alpha = softmax(m, block_max)
    block_max = logits(1, mask_offset=tk)

    def step(visit, carry):
        m, alpha_prev, block_max = carry
        accumulate(visit - 1, alpha_prev)
        m, alpha = softmax(m, block_max)
        return m, alpha, logits(visit + 1)

    def step_pair(t, carry):
        return step(2 * t + 2, step(2 * t + 1, carry))

    m, alpha, block_max = lax.fori_loop(0, i, step_pair, (m, alpha, block_max))
    accumulate(n_visits - 2, alpha)
    m, alpha = softmax(m, block_max)
    accumulate(n_visits - 1, alpha)

    d = ATTN_HEAD_DIM
    for c in range(hb):
        pair = []
        for j in range(HEADS_PER_BLOCK):
            acc = acc_scr[c * HEADS_PER_BLOCK + j]
            pair.append(acc[:d] / acc[d:d + 1])
        o_ref[0, :, c * LANES:(c + 1) * LANES] = jnp.concatenate(pair, axis=0).T.astype(BF16)


def _flash(qt, qat, k, ka, vt):
    B, S, _ = k.shape
    tq, tk, hb = ATTN_TQ, ATTN_TK, ATTN_HEAD_BLOCKS_PER_STEP
    n_heads = hb * HEADS_PER_BLOCK
    q_blk = pl.BlockSpec((1, hb * LANES, tq), lambda b, h, i: (b, h, i))
    q_slots = pl.BlockSpec((1, LANES, tq), lambda b, h, i: (b, 0, i))
    k_blk = pl.BlockSpec((1, S, hb * LANES), lambda b, h, i: (b, 0, h))
    k_slots = pl.BlockSpec((1, S, LANES), lambda b, h, i: (b, 0, 0))
    v_blk = pl.BlockSpec((1, hb, S // tk, LANES, tk), lambda b, h, i: (b, h, 0, 0, 0))
    return pl.pallas_call(
        functools.partial(_flash_kernel, tq=tq, tk=tk, hb=hb),
        grid=(B, HEAD_BLOCKS // hb, S // tq),
        in_specs=[q_blk, q_slots, k_blk, k_slots, v_blk],
        out_specs=pl.BlockSpec((1, tq, hb * LANES), lambda b, h, i: (b, i, h)),
        out_shape=jax.ShapeDtypeStruct((B, S, ATTN_WIDTH), BF16),
        scratch_shapes=[
            pltpu.VMEM((n_heads, tk, tq), F32),
            pltpu.VMEM((n_heads, tk, tq), BF16),
            pltpu.VMEM((n_heads, ACC_ROWS, tq), F32),
        ],
        compiler_params=pltpu.CompilerParams(
            dimension_semantics=("arbitrary", "arbitrary", "arbitrary"), vmem_limit_bytes=VMEM_LIMIT),
        name="fox_flash",
    )(qt, qat, k, ka, vt)


def _attn_out_kernel(o_ref, g_ref, x_ref, w_ref, lg_ref, lb_ref, y_ref):
    y = o_ref[0].astype(F32) * _silu(g_ref[0].astype(F32))
    h = jnp.dot(y.astype(BF16), w_ref[...], preferred_element_type=F32)
    y_ref[0] = _deepnorm_ln(x_ref[0], h, lg_ref[...], lb_ref[...])


def _attn_out(o, gate, x, w_out, ln_g, ln_b):
    B, S, D = x.shape
    tm = OUT_ROWS
    const = lambda b, s: (0, 0)
    row_blk = lambda b, s: (b, s, 0)
    return pl.pallas_call(
        _attn_out_kernel,
        grid=(B, S // tm),
        in_specs=[
            pl.BlockSpec((1, tm, ATTN_WIDTH), row_blk),
            pl.BlockSpec((1, tm, ATTN_WIDTH), row_blk),
            pl.BlockSpec((1, tm, D), row_blk),
            pl.BlockSpec(w_out.shape, const),
            pl.BlockSpec(ln_g.shape, const),
            pl.BlockSpec(ln_b.shape, const),
        ],
        out_specs=pl.BlockSpec((1, tm, D), row_blk),
        out_shape=jax.ShapeDtypeStruct((B, S, D), F32),
        compiler_params=pltpu.CompilerParams(
            dimension_semantics=("arbitrary", "arbitrary"), vmem_limit_bytes=VMEM_LIMIT),
        name="attn_out",
    )(o, gate, x, w_out, ln_g, ln_b)


def _chunk_interleave(tm):
    steps = tm // SUBLANES
    p = np.zeros((tm, tm), np.float32)
    for s in range(SUBLANES):
        for r in range(steps):
            p[r * SUBLANES + s, s * steps + r] = 1.0
    return p


def _rnn_kernel(x_ref, *refs, tm, nb):
    consts, (y_ref, tail_ref, h_ref) = refs[:-3], refs[-3:]
    si = pl.program_id(1)

    @pl.when(si == 0)
    def _():
        tail_ref[...] = jnp.zeros_like(tail_ref)
        h_ref[...] = jnp.zeros_like(h_ref)

    tiles = [_rnn_tile(x_ref.at[bb], *consts, y_ref.at[bb], tail_ref.at[bb], h_ref.at[bb], tm=tm)
             for bb in range(nb)]
    order = sorted(((RNN_STAGE_SKEW * bb + stage, -bb, bb) for bb in range(nb) for stage in range(RNN_STAGES)))
    for _, _, bb in order:
        next(tiles[bb], None)


def _rnn_tile(x_ref, perm_ref, unperm_ref, win_ref, cw_ref, cb_ref, wa_ref, ba_ref, wi_ref, bi_ref, lam_ref,
              wout_ref, lg_ref, lb_ref, y_ref, tail_ref, h_ref, *, tm):
    w = RNN_WIDTH
    steps = tm // SUBLANES
    x = x_ref[...]
    xb = jnp.dot(perm_ref[...], x.astype(BF16), preferred_element_type=F32).astype(BF16)
    u = jnp.dot(xb, win_ref[:, 0:w], preferred_element_type=F32)
    gate_half = jnp.dot(xb, win_ref[:, w:2 * w], preferred_element_type=F32)
    yield

    n_tail = (CONV_WIDTH - 1) * SUBLANES
    row8 = lax.broadcasted_iota(jnp.int32, (SUBLANES, w), 0)
    wrapped = []
    for g in range(CONV_WIDTH - 1):
        lo = tm - n_tail + g * SUBLANES
        here = pltpu.roll(u[lo:lo + SUBLANES], 1, 0)
        before = pltpu.roll(tail_ref[g * SUBLANES:(g + 1) * SUBLANES, :], 1, 0)
        wrapped.append(jnp.where(row8 == 0, before, here))
    conv_half = cb_ref[...] + u * cw_ref[CONV_WIDTH - 1:CONV_WIDTH, :]
    for back in range(1, CONV_WIDTH):
        shifted = jnp.concatenate(wrapped[CONV_WIDTH - 1 - back:] + [u[:tm - back * SUBLANES]], axis=0)
        tap = CONV_WIDTH - 1 - back
        conv_half = conv_half + shifted * cw_ref[tap:tap + 1, :]
    tail_ref[...] = u[tm - n_tail:tm]

    cbf = conv_half.astype(BF16)
    bw = RNN_BLOCK_WIDTH
    ra = jnp.concatenate(
        [jnp.dot(cbf[:, n * bw:(n + 1) * bw], wa_ref[n], preferred_element_type=F32) for n in range(RNN_BLOCKS)],
        axis=1)
    ri = jnp.concatenate(
        [jnp.dot(cbf[:, n * bw:(n + 1) * bw], wi_ref[n], preferred_element_type=F32) for n in range(RNN_BLOCKS)],
        axis=1)
    yield
    tanh_r = jnp.tanh(ra + ba_ref[...])
    tanh_i = jnp.tanh(ri + bi_ref[...])
    c_half = (-0.5 * LRU_C) * _softplus(-lam_ref[...])
    log_a = c_half * tanh_r + c_half
    a = jnp.exp(log_a)
    gain2 = jnp.tanh(-log_a) * (a * a + 1.0)
    gain = gain2 * lax.rsqrt(jnp.maximum(gain2, F32_MIN_NORMAL))
    b = gain * ((tanh_i + 1.0) * conv_half)

    h_loc = jnp.zeros((SUBLANES, w), F32)
    a_cum = jnp.ones((SUBLANES, w), F32)
    h_locs, a_cums = [], []
    for step in range(steps):
        a_step = a[step * SUBLANES:(step + 1) * SUBLANES]
        h_loc = a_step * h_loc + b[step * SUBLANES:(step + 1) * SUBLANES]
        a_cum = a_step * a_cum
        h_locs.append(h_loc)
        a_cums.append(a_cum)
    state = h_ref[0:1, :]
    entering = []
    for s in range(SUBLANES):
        entering.append(state)
        state = a_cum[s:s + 1] * state + h_loc[s:s + 1]
    h_ref[0:1, :] = state
    h_in = jnp.concatenate(entering, axis=0)
    h = jnp.concatenate([h_locs[step] + a_cums[step] * h_in for step in range(steps)], axis=0)

    y = (h * (gate_half * (jnp.tanh(gate_half) + 1.0))).astype(BF16)
    yield
    y = jnp.dot(unperm_ref[...], y, preferred_element_type=F32).astype(BF16)
    out = jnp.dot(y, wout_ref[...], preferred_element_type=F32)
    yield
    y_ref[...] = _deepnorm_ln(x, out, lg_ref[...], lb_ref[...])


def _rnn_layer(x, w_in, conv_w, conv_b, w_a, b_a, w_i, b_i, lam, w_out, ln_g, ln_b):
    B, S, D = x.shape
    tm = RNN_ROWS
    row_blk = lambda b, s: (b, s, 0)

    def const(arr):
        nd = arr.ndim
        return pl.BlockSpec(arr.shape, lambda b, s: (0,) * nd)

    perm = _chunk_interleave(tm)
    consts = [jnp.asarray(perm, BF16), jnp.asarray(perm.T, BF16),
              w_in, conv_w, conv_b, w_a, b_a, w_i, b_i, lam, w_out, ln_g, ln_b]
    nb = RNN_BATCH_ROWS
    assert B % nb == 0
    return pl.pallas_call(
        functools.partial(_rnn_kernel, tm=tm, nb=nb),
        grid=(B // nb, S // tm),
        in_specs=[pl.BlockSpec((nb, tm, D), row_blk)] + [const(c) for c in consts],
        out_specs=pl.BlockSpec((nb, tm, D), row_blk),
        out_shape=jax.ShapeDtypeStruct((B, S, D), F32),
        scratch_shapes=[pltpu.VMEM((nb, (CONV_WIDTH - 1) * SUBLANES, RNN_WIDTH), F32),
                        pltpu.VMEM((nb, SUBLANES, RNN_WIDTH), F32)],
        compiler_params=pltpu.CompilerParams(
            dimension_semantics=("arbitrary", "arbitrary"), vmem_limit_bytes=VMEM_LIMIT),
        name="rglru_layer",
    )(x, *consts)


def _row(v):
    return v.reshape(1, -1).astype(F32)


def kernel(x, ln_g, ln_b, attn_w_in, attn_b_f, attn_w_out, rnn_w_in, rnn_conv_w, rnn_conv_b,
           rnn_w_a, rnn_b_a, rnn_w_i, rnn_b_i, rnn_lambda, rnn_w_out):
    assert x.shape[-1] == D_MODEL and x.shape[1] % ATTN_TQ == 0
    w = ATTN_WIDTH
    for layer in range(DEPTH):
        idx = layer // N_MIXERS
        g, b = _row(ln_g[layer]), _row(ln_b[layer])
        if layer % N_MIXERS == 0:
            w_in = attn_w_in[idx].astype(BF16)
            w_f = jnp.pad(w_in[:, 4 * w:], ((0, 0), (0, LANES - ATTN_HEADS)))
            b_f = jnp.pad(attn_b_f[idx].astype(F32), (0, LANES - ATTN_HEADS)).reshape(1, LANES)
            qt, qat, k, ka, vt, gate = _attn_proj(
                x, w_in[:, w:2 * w], w_in[:, 3 * w:4 * w], w_in[:, 0:w].T, w_in[:, 2 * w:3 * w].T, w_f, b_f)
            o = _flash(qt, qat, k, ka, vt)
            x = _attn_out(o, gate, x, attn_w_out[idx].astype(BF16), g, b)
        else:
            w_in = rnn_w_in[idx] * jnp.where(jnp.arange(2 * RNN_WIDTH) < RNN_WIDTH, 1.0, 0.5)
            x = _rnn_layer(
                x, w_in.astype(BF16), 0.5 * rnn_conv_w[idx].astype(F32), 0.5 * _row(rnn_conv_b[idx]),
                rnn_w_a[idx].astype(BF16), 0.5 * _row(rnn_b_a[idx]),
                rnn_w_i[idx].astype(BF16), 0.5 * _row(rnn_b_i[idx]),
                _row(rnn_lambda[idx]), rnn_w_out[idx].astype(BF16), g, b)
    return x
```

```python
import functools

import jax
import jax.numpy as jnp
import numpy as np
from jax import lax
from jax.experimental import pallas as pl
from jax.experimental.pallas import tpu as pltpu

F32 = jnp.float32
BF16 = jnp.bfloat16

D_MODEL = 1024
DEPTH = 4
N_MIXERS = 2
ATTN_HEADS = 16
ATTN_HEAD_DIM = 64
ATTN_WIDTH = ATTN_HEADS * ATTN_HEAD_DIM
RNN_WIDTH = D_MODEL
RNN_BLOCK_WIDTH = 256
RNN_BLOCKS = RNN_WIDTH // RNN_BLOCK_WIDTH
CONV_WIDTH = 4
LRU_C = 8.0
DEEPNORM_ALPHA = (2.0 * DEPTH) ** 0.25
LN_EPS = 1e-5
LOG2_E = 1.4426950408889634
F32_MIN_NORMAL = float(np.finfo(np.float32).tiny)

LANES = 128
SUBLANES = 8
HEADS_PER_BLOCK = LANES // ATTN_HEAD_DIM
HEAD_BLOCKS = ATTN_HEADS // HEADS_PER_BLOCK
AUG_STRIDE = 8
CUM_PARTS = 3
ONES_LANE = ATTN_HEADS

VMEM_LIMIT = 56 * 1024 * 1024

PROJ_ROWS = 512
OUT_ROWS = 512
RNN_ROWS = 256
RNN_BATCH_ROWS = 2
RNN_STAGES = 5
RNN_STAGE_SKEW = 2
ATTN_TK = 256
ATTN_TQ = 4 * ATTN_TK
ATTN_HEAD_BLOCKS_PER_STEP = 2
BF16_SUBLANES = 2 * SUBLANES
ACC_ROWS = ATTN_HEAD_DIM + BF16_SUBLANES


def _softplus(x):
    return jnp.maximum(x, 0.0) + jnp.log1p(jnp.exp(-jnp.abs(x)))


def _log_sigmoid(x):
    return jnp.minimum(x, 0.0) - jnp.log1p(jnp.exp(-jnp.abs(x)))


def _sigmoid(x):
    return 0.5 * jnp.tanh(0.5 * x) + 0.5


def _silu(x):
    return x * _sigmoid(x)


def _split_bf16(x):
    hi = x.astype(BF16)
    r1 = x - hi.astype(F32)
    mid = r1.astype(BF16)
    lo = (r1 - mid.astype(F32)).astype(BF16)
    return hi, mid, lo


def _deepnorm_ln(x, h, g, b):
    z = DEEPNORM_ALPHA * x + h
    mu = jnp.mean(z, axis=-1, keepdims=True)
    zc = z - mu
    var = jnp.mean(zc * zc, axis=-1, keepdims=True)
    return zc * lax.rsqrt(var + LN_EPS) * g + b


def _dot_nt(a, b):
    return lax.dot_general(a, b, (((1,), (1,)), ((), ())), preferred_element_type=F32)


def _attn_proj_kernel(x_ref, wk_ref, wg_ref, wqt_ref, wvt_ref, wf_ref, bf_ref, tri_ref, eqt_ref, ek_ref,
                      qt_ref, qat_ref, k_ref, ka_ref, vt_ref, g_ref, carry_ref, *, tm, tk):
    si = pl.program_id(1)

    @pl.when(si == 0)
    def _():
        carry_ref[...] = jnp.zeros_like(carry_ref)

    xb = x_ref[0].astype(BF16)
    k_ref[0] = jnp.dot(xb, wk_ref[...], preferred_element_type=F32).astype(BF16)
    g_ref[0] = jnp.dot(xb, wg_ref[...], preferred_element_type=F32).astype(BF16)
    qt_ref[0] = (_dot_nt(wqt_ref[...], xb) * (ATTN_HEAD_DIM ** -0.5 * LOG2_E)).astype(BF16)
    vt = _dot_nt(wvt_ref[...], xb).astype(BF16)
    for hb in range(HEAD_BLOCKS):
        for c in range(tm // tk):
            vt_ref[0, hb, c] = vt[hb * LANES:(hb + 1) * LANES, c * tk:(c + 1) * tk]

    f = jnp.dot(xb, wf_ref[...], preferred_element_type=F32) + bf_ref[...]
    ls = _log_sigmoid(f)
    tri = tri_ref[...]
    cum = carry_ref[0:1, :]
    for part in _split_bf16(ls):
        cum = cum + jnp.dot(tri, part, preferred_element_type=F32)
    carry_ref[0:1, :] = cum[tm - 1:tm, :]

    hi, mid, lo = _split_bf16(cum * LOG2_E)
    lane = lax.broadcasted_iota(jnp.int32, hi.shape, 1)
    hi = jnp.where(lane == ONES_LANE, jnp.ones_like(hi), hi)
    parts = jnp.concatenate([hi, mid, lo], axis=1)
    qat_ref[0] = _dot_nt(eqt_ref[...], parts).astype(BF16)
    ka_ref[0] = jnp.dot(parts, ek_ref[...], preferred_element_type=F32).astype(BF16)


def _bias_expanders():
    assert ATTN_HEADS * AUG_STRIDE == LANES
    eq = np.zeros((CUM_PARTS * LANES, LANES), np.float32)
    ek = np.zeros_like(eq)
    for h in range(ATTN_HEADS):
        base = h * AUG_STRIDE
        for p in range(CUM_PARTS):
            eq[p * LANES + h, base + p] = 1.0
            ek[ONES_LANE, base + p] = 1.0
            ek[p * LANES + h, base + CUM_PARTS + p] = -1.0
            eq[ONES_LANE, base + CUM_PARTS + p] = 1.0
    return jnp.asarray(eq.T, BF16), jnp.asarray(ek, BF16)


def _attn_proj(x, w_k, w_g, w_qt, w_vt, w_f, b_f):
    B, S, D = x.shape
    tm, tk = PROJ_ROWS, ATTN_TK
    tri = jnp.asarray(np.tril(np.ones((tm, tm), np.float32)), BF16)
    eqt, ek = _bias_expanders()
    consts = [w_k, w_g, w_qt, w_vt, w_f, b_f, tri, eqt, ek]
    const = lambda b, s: (0, 0)
    row_blk = lambda b, s: (b, s, 0)
    col_blk = lambda b, s: (b, 0, s)
    s_major = pl.BlockSpec((1, tm, ATTN_WIDTH), row_blk)
    f_major = pl.BlockSpec((1, ATTN_WIDTH, tm), col_blk)
    vt_blk = pl.BlockSpec((1, HEAD_BLOCKS, tm // tk, LANES, tk), lambda b, s: (b, 0, s, 0, 0))
    s_shape = jax.ShapeDtypeStruct((B, S, ATTN_WIDTH), BF16)
    f_shape = jax.ShapeDtypeStruct((B, ATTN_WIDTH, S), BF16)
    vt_shape = jax.ShapeDtypeStruct((B, HEAD_BLOCKS, S // tk, LANES, tk), BF16)
    s_slots = pl.BlockSpec((1, tm, LANES), row_blk)
    f_slots = pl.BlockSpec((1, LANES, tm), col_blk)
    s_slots_shape = jax.ShapeDtypeStruct((B, S, LANES), BF16)
    f_slots_shape = jax.ShapeDtypeStruct((B, LANES, S), BF16)
    return pl.pallas_call(
        functools.partial(_attn_proj_kernel, tm=tm, tk=tk),
        grid=(B, S // tm),
        in_specs=[pl.BlockSpec((1, tm, D), row_blk)] + [pl.BlockSpec(c.shape, const) for c in consts],
        out_specs=[f_major, f_slots, s_major, s_slots, vt_blk, s_major],
        out_shape=[f_shape, f_slots_shape, s_shape, s_slots_shape, vt_shape, s_shape],
        scratch_shapes=[pltpu.VMEM((SUBLANES, LANES), F32)],
        compiler_params=pltpu.CompilerParams(
            dimension_semantics=("arbitrary", "arbitrary"), vmem_limit_bytes=VMEM_LIMIT),
        name="attn_proj",
    )(x, *consts)


def _flash_kernel(qt_ref, qat_ref, k_ref, ka_ref, vt_ref, o_ref, s_scr, p_scr, acc_scr, *, tq, tk, hb):
    i = pl.program_id(2)
    ratio = tq // tk
    n_visits = ratio * i + ratio
    n_heads = hb * HEADS_PER_BLOCK
    qrow = lax.broadcasted_iota(jnp.int32, (2 * LANES, tq), 0)
    ones = jnp.ones((ACC_ROWS - ATTN_HEAD_DIM, tk), BF16)

    first_head = pl.program_id(1) * n_heads
    ka_all = ka_ref.at[0]
    q_heads = []
    for c in range(hb):
        q_all = jnp.concatenate([qt_ref[0, c * LANES:(c + 1) * LANES], qat_ref[0]], axis=0)
        for j in range(HEADS_PER_BLOCK):
            lo = j * ATTN_HEAD_DIM
            slot = LANES + (first_head + c * HEADS_PER_BLOCK + j) * AUG_STRIDE
            mine = ((qrow >= lo) & (qrow < lo + ATTN_HEAD_DIM)) | ((qrow >= slot) & (qrow < slot + AUG_STRIDE))
            q_heads.append(jnp.where(mine, q_all, jnp.zeros_like(q_all)))

    def block_of(visit, diagonal):
        return ratio * i + visit if diagonal else ratio * i + ratio - 1 - visit

    def logits(visit, diagonal=False):
        lo = visit * tk if diagonal else 0
        ks = pl.multiple_of(block_of(visit, diagonal) * tk, tk)
        maxima = []
        ka = ka_all[pl.ds(ks, tk), :]
        for c in range(hb):
            kk = jnp.concatenate([k_ref[0, pl.ds(ks, tk), c * LANES:(c + 1) * LANES], ka], axis=1)
            for j in range(HEADS_PER_BLOCK):
                h = c * HEADS_PER_BLOCK + j
                s = jnp.dot(kk, q_heads[h][:, lo:], preferred_element_type=F32)
                if diagonal:
                    key = lax.broadcasted_iota(jnp.int32, s.shape, 0)
                    qry = lax.broadcasted_iota(jnp.int32, s.shape, 1)
                    s = jnp.where(key <= qry, s, -jnp.inf)
                s_scr[h, :, lo:] = s
                maxima.append(jnp.broadcast_to(jnp.max(s, axis=0, keepdims=True), (SUBLANES, s.shape[1])))
        return tuple(maxima)

    def softmax(m, block_max, lo=0):
        m_out, alpha_out = [], []
        for h in range(n_heads):
            m_old = m[h][:, lo:]
            m_new = jnp.maximum(m_old, block_max[h])
            alpha_out.append(jnp.exp2(m_old - m_new))
            p_scr[h, :, lo:] = jnp.exp2(s_scr[h, :, lo:] - m_new[0:1]).astype(BF16)
            m_out.append(m_new if lo == 0 else jnp.concatenate([m[h][:, :lo], m_new], axis=1))
        return tuple(m_out), tuple(alpha_out)

    def accumulate(visit, alpha, diagonal=False):
        lo = visit * tk if diagonal else 0
        kb = block_of(visit, diagonal)
        for c in range(hb):
            vt = vt_ref[0, c, kb]
            for j in range(HEADS_PER_BLOCK):
                h = c * HEADS_PER_BLOCK + j
                v1 = jnp.concatenate([vt[j * ATTN_HEAD_DIM:(j + 1) * ATTN_HEAD_DIM], ones], axis=0)
                acc_scr[h, :, lo:] = alpha[h][0:1] * acc_scr[h, :, lo:] + jnp.dot(
                    v1, p_scr[h, :, lo:], preferred_element_type=F32)

    acc_scr[...] = jnp.zeros_like(acc_scr)
    m = tuple(jnp.full((SUBLANES, tq), -jnp.inf, F32) for _ in range(n_heads))

    block_max = logits(0, diagonal=True)
    m, alpha = softmax(m, block_max)
    block_max = logits(1, diagonal=True)
    for v in range(1, ratio - 1):
        accumulate(v - 1, alpha, diagonal=True)
        m, alpha = softmax(m, block_max, lo=v * tk)
        block_max = logits(v + 1, diagonal=True)
    last = ratio - 1

    @pl.when(i == 0)
    def _():
        accumulate(last - 1, alpha, diagonal=True)
        _, alpha_last = softmax(m, block_max, lo=last * tk)
        accumulate(last, alpha_last, diagonal=True)

    @pl.when(i > 0)
    def _():
        accumulate(last - 1, alpha, diagonal=True)
        m1, alpha1 = softmax(m, block_max, lo=last * tk)
        block_max1 = logits(ratio)
        accumulate(last, alpha1, diagonal=True)
        m2, alpha2 = softmax(m1, block_max1)
        carry = (m2, alpha2, logits(ratio + 1))

        def step(visit, carry):
            m, alpha_prev, block_max = carry
            accumulate(visit - 1, alpha_prev)
            m, alpha = softmax(m, block_max)
            return m, alpha, logits(visit + 1)

        def step_pair(t, carry):
            visit = ratio + 1 + 2 * t
            return step(visit + 1, step(visit, carry))

        m3, alpha3, block_max3 = lax.fori_loop(0, (ratio * i - 2) // 2, step_pair, carry)
        accumulate(n_visits - 2, alpha3)
        _, alpha_last = softmax(m3, block_max3)
        accumulate(n_visits - 1, alpha_last)

    d = ATTN_HEAD_DIM
    for c in range(hb):
        pair = []
        for j in range(HEADS_PER_BLOCK):
            acc = acc_scr[c * HEADS_PER_BLOCK + j]
            pair.append(acc[:d] / acc[d:d + 1])
        o_ref[0, :, c * LANES:(c + 1) * LANES] = jnp.concatenate(pair, axis=0).T.astype(BF16)


def _flash(qt, qat, k, ka, vt):
    B, S, _ = k.shape
    tq, tk, hb = ATTN_TQ, ATTN_TK, ATTN_HEAD_BLOCKS_PER_STEP
    assert tq % (2 * tk) == 0 and S % tq == 0
    n_heads = hb * HEADS_PER_BLOCK
    q_blk = pl.BlockSpec((1, hb * LANES, tq), lambda b, h, i: (b, h, i))
    q_slots = pl.BlockSpec((1, LANES, tq), lambda b, h, i: (b, 0, i))
    k_blk = pl.BlockSpec((1, S, hb * LANES), lambda b, h, i: (b, 0, h))
    k_slots = pl.BlockSpec((1, S, LANES), lambda b, h, i: (b, 0, 0))
    v_blk = pl.BlockSpec((1, hb, S // tk, LANES, tk), lambda b, h, i: (b, h, 0, 0, 0))
    return pl.pallas_call(
        functools.partial(_flash_kernel, tq=tq, tk=tk, hb=hb),
        grid=(B, HEAD_BLOCKS // hb, S // tq),
        in_specs=[q_blk, q_slots, k_blk, k_slots, v_blk],
        out_specs=pl.BlockSpec((1, tq, hb * LANES), lambda b, h, i: (b, i, h)),
        out_shape=jax.ShapeDtypeStruct((B, S, ATTN_WIDTH), BF16),
        scratch_shapes=[
            pltpu.VMEM((n_heads, tk, tq), F32),
            pltpu.VMEM((n_heads, tk, tq), BF16),
            pltpu.VMEM((n_heads, ACC_ROWS, tq), F32),
        ],
        compiler_params=pltpu.CompilerParams(
            dimension_semantics=("arbitrary", "arbitrary", "arbitrary"), vmem_limit_bytes=VMEM_LIMIT),
        name="fox_flash",
    )(qt, qat, k, ka, vt)


def _attn_out_kernel(o_ref, g_ref, x_ref, w_ref, lg_ref, lb_ref, y_ref):
    y = o_ref[0].astype(F32) * _silu(g_ref[0].astype(F32))
    h = jnp.dot(y.astype(BF16), w_ref[...], preferred_element_type=F32)
    y_ref[0] = _deepnorm_ln(x_ref[0], h, lg_ref[...], lb_ref[...])


def _attn_out(o, gate, x, w_out, ln_g, ln_b):
    B, S, D = x.shape
    tm = OUT_ROWS
    const = lambda b, s: (0, 0)
    row_blk = lambda b, s: (b, s, 0)
    return pl.pallas_call(
        _attn_out_kernel,
        grid=(B, S // tm),
        in_specs=[
            pl.BlockSpec((1, tm, ATTN_WIDTH), row_blk),
            pl.BlockSpec((1, tm, ATTN_WIDTH), row_blk),
            pl.BlockSpec((1, tm, D), row_blk),
            pl.BlockSpec(w_out.shape, const),
            pl.BlockSpec(ln_g.shape, const),
            pl.BlockSpec(ln_b.shape, const),
        ],
        out_specs=pl.BlockSpec((1, tm, D), row_blk),
        out_shape=jax.ShapeDtypeStruct((B, S, D), F32),
        compiler_params=pltpu.CompilerParams(
            dimension_semantics=("arbitrary", "arbitrary"), vmem_limit_bytes=VMEM_LIMIT),
        name="attn_out",
    )(o, gate, x, w_out, ln_g, ln_b)


def _chunk_interleave(tm):
    steps = tm // SUBLANES
    p = np.zeros((tm, tm), np.float32)
    for s in range(SUBLANES):
        for r in range(steps):
            p[r * SUBLANES + s, s * steps + r] = 1.0
    return p


def _rnn_kernel(x_ref, *refs, tm, nb):
    consts, (y_ref, tail_ref, h_ref) = refs[:-3], refs[-3:]
    si = pl.program_id(1)

    @pl.when(si == 0)
    def _():
        tail_ref[...] = jnp.zeros_like(tail_ref)
        h_ref[...] = jnp.zeros_like(h_ref)

    tiles = [_rnn_tile(x_ref.at[bb], *consts, y_ref.at[bb], tail_ref.at[bb], h_ref.at[bb], tm=tm)
             for bb in range(nb)]
    order = sorted(((RNN_STAGE_SKEW * bb + stage, -bb, bb) for bb in range(nb) for stage in range(RNN_STAGES)))
    for _, _, bb in order:
        next(tiles[bb], None)


def _rnn_tile(x_ref, perm_ref, unperm_ref, win_ref, cw_ref, cb_ref, wa_ref, ba_ref, wi_ref, bi_ref, lam_ref,
              wout_ref, lg_ref, lb_ref, y_ref, tail_ref, h_ref, *, tm):
    w = RNN_WIDTH
    steps = tm // SUBLANES
    x = x_ref[...]
    xb = jnp.dot(perm_ref[...], x.astype(BF16), preferred_element_type=F32).astype(BF16)
    u = jnp.dot(xb, win_ref[:, 0:w], preferred_element_type=F32)
    gate_half = jnp.dot(xb, win_ref[:, w:2 * w], preferred_element_type=F32)
    yield

    n_tail = (CONV_WIDTH - 1) * SUBLANES
    row8 = lax.broadcasted_iota(jnp.int32, (SUBLANES, w), 0)
    wrapped = []
    for g in range(CONV_WIDTH - 1):
        lo = tm - n_tail + g * SUBLANES
        here = pltpu.roll(u[lo:lo + SUBLANES], 1, 0)
        before = pltpu.roll(tail_ref[g * SUBLANES:(g + 1) * SUBLANES, :], 1, 0)
        wrapped.append(jnp.where(row8 == 0, before, here))
    conv_half = cb_ref[...] + u * cw_ref[CONV_WIDTH - 1:CONV_WIDTH, :]
    for back in range(1, CONV_WIDTH):
        shifted = jnp.concatenate(wrapped[CONV_WIDTH - 1 - back:] + [u[:tm - back * SUBLANES]], axis=0)
        tap = CONV_WIDTH - 1 - back
        conv_half = conv_half + shifted * cw_ref[tap:tap + 1, :]
    tail_ref[...] = u[tm - n_tail:tm]

    cbf = conv_half.astype(BF16)
    bw = RNN_BLOCK_WIDTH
    ra = jnp.concatenate(
        [jnp.dot(cbf[:, n * bw:(n + 1) * bw], wa_ref[n], preferred_element_type=F32) for n in range(RNN_BLOCKS)],
        axis=1)
    ri = jnp.concatenate(
        [jnp.dot(cbf[:, n * bw:(n + 1) * bw], wi_ref[n], preferred_element_type=F32) for n in range(RNN_BLOCKS)],
        axis=1)
    yield
    tanh_r = jnp.tanh(ra + ba_ref[...])
    tanh_i = jnp.tanh(ri + bi_ref[...])
    c_half = (-0.5 * LRU_C) * _softplus(-lam_ref[...])
    log_a = c_half * tanh_r + c_half
    a = jnp.exp(log_a)
    gain2 = jnp.tanh(-log_a) * (a * a + 1.0)
    gain = gain2 * lax.rsqrt(jnp.maximum(gain2, F32_MIN_NORMAL))
    b = gain * ((tanh_i + 1.0) * conv_half)

    h_loc = jnp.zeros((SUBLANES, w), F32)
    a_cum = jnp.ones((SUBLANES, w), F32)
    h_locs, a_cums = [], []
    for step in range(steps):
        a_step = a[step * SUBLANES:(step + 1) * SUBLANES]
        h_loc = a_step * h_loc + b[step * SUBLANES:(step + 1) * SUBLANES]
        a_cum = a_step * a_cum
        h_locs.append(h_loc)
        a_cums.append(a_cum)
    state = h_ref[0:1, :]
    entering = []
    for s in range(SUBLANES):
        entering.append(state)
        state = a_cum[s:s + 1] * state + h_loc[s:s + 1]
    h_ref[0:1, :] = state
    h_in = jnp.concatenate(entering, axis=0)
    h = jnp.concatenate([h_locs[step] + a_cums[step] * h_in for step in range(steps)], axis=0)

    y = (h * (gate_half * (jnp.tanh(gate_half) + 1.0))).astype(BF16)
    yield
    y = jnp.dot(unperm_ref[...], y, preferred_element_type=F32).astype(BF16)
    out = jnp.dot(y, wout_ref[...], preferred_element_type=F32)
    yield
    y_ref[...] = _deepnorm_ln(x, out, lg_ref[...], lb_ref[...])


def _rnn_layer(x, w_in, conv_w, conv_b, w_a, b_a, w_i, b_i, lam, w_out, ln_g, ln_b):
    B, S, D = x.shape
    tm = RNN_ROWS
    row_blk = lambda b, s: (b, s, 0)

    def const(arr):
        nd = arr.ndim
        return pl.BlockSpec(arr.shape, lambda b, s: (0,) * nd)

    perm = _chunk_interleave(tm)
    consts = [jnp.asarray(perm, BF16), jnp.asarray(perm.T, BF16),
              w_in, conv_w, conv_b, w_a, b_a, w_i, b_i, lam, w_out, ln_g, ln_b]
    nb = RNN_BATCH_ROWS
    assert B % nb == 0
    return pl.pallas_call(
        functools.partial(_rnn_kernel, tm=tm, nb=nb),
        grid=(B // nb, S // tm),
        in_specs=[pl.BlockSpec((nb, tm, D), row_blk)] + [const(c) for c in consts],
        out_specs=pl.BlockSpec((nb, tm, D), row_blk),
        out_shape=jax.ShapeDtypeStruct((B, S, D), F32),
        scratch_shapes=[pltpu.VMEM((nb, (CONV_WIDTH - 1) * SUBLANES, RNN_WIDTH), F32),
                        pltpu.VMEM((nb, SUBLANES, RNN_WIDTH), F32)],
        compiler_params=pltpu.CompilerParams(
            dimension_semantics=("arbitrary", "arbitrary"), vmem_limit_bytes=VMEM_LIMIT),
        name="rglru_layer",
    )(x, *consts)


def _row(v):
    return v.reshape(1, -1).astype(F32)


def kernel(x, ln_g, ln_b, attn_w_in, attn_b_f, attn_w_out, rnn_w_in, rnn_conv_w, rnn_conv_b,
           rnn_w_a, rnn_b_a, rnn_w_i, rnn_b_i, rnn_lambda, rnn_w_out):
    assert x.shape[-1] == D_MODEL and x.shape[1] % ATTN_TQ == 0
    w = ATTN_WIDTH
    for layer in range(DEPTH):
        idx = layer // N_MIXERS
        g, b = _row(ln_g[layer]), _row(ln_b[layer])
        if layer % N_MIXERS == 0:
            w_in = attn_w_in[idx].astype(BF16)
            w_f = jnp.pad(w_in[:, 4 * w:], ((0, 0), (0, LANES - ATTN_HEADS)))
            b_f = jnp.pad(attn_b_f[idx].astype(F32), (0, LANES - ATTN_HEADS)).reshape(1, LANES)
            qt, qat, k, ka, vt, gate = _attn_proj(
                x, w_in[:, w:2 * w], w_in[:, 3 * w:4 * w], w_in[:, 0:w].T, w_in[:, 2 * w:3 * w].T, w_f, b_f)
            o = _flash(qt, qat, k, ka, vt)
            x = _attn_out(o, gate, x, attn_w_out[idx].astype(BF16), g, b)
        else:
            w_in = rnn_w_in[idx] * jnp.where(jnp.arange(2 * RNN_WIDTH) < RNN_WIDTH, 1.0, 0.5)
            x = _rnn_layer(
                x, w_in.astype(BF16), 0.5 * rnn_conv_w[idx].astype(F32), 0.5 * _row(rnn_conv_b[idx]),
                rnn_w_a[idx].astype(BF16), 0.5 * _row(rnn_b_a[idx]),
                rnn_w_i[idx].astype(BF16), 0.5 * _row(rnn_b_i[idx]),
                _row(rnn_lambda[idx]), rnn_w_out[idx].astype(BF16), g, b)
    return x
```

```python
import functools

import jax
import jax.numpy as jnp
import numpy as np
from jax import lax
from jax.experimental import pallas as pl
from jax.experimental.pallas import tpu as pltpu

F32 = jnp.float32
BF16 = jnp.bfloat16

D_MODEL = 1024
DEPTH = 4
N_MIXERS = 2
ATTN_HEADS = 16
ATTN_HEAD_DIM = 64
ATTN_WIDTH = ATTN_HEADS * ATTN_HEAD_DIM
RNN_WIDTH = D_MODEL
RNN_BLOCK_WIDTH = 256
RNN_BLOCKS = RNN_WIDTH // RNN_BLOCK_WIDTH
CONV_WIDTH = 4
LRU_C = 8.0
DEEPNORM_ALPHA = (2.0 * DEPTH) ** 0.25
LN_EPS = 1e-5
LOG2_E = 1.4426950408889634
F32_MIN_NORMAL = float(np.finfo(np.float32).tiny)

LANES = 128
SUBLANES = 8
HEADS_PER_BLOCK = LANES // ATTN_HEAD_DIM
HEAD_BLOCKS = ATTN_HEADS // HEADS_PER_BLOCK
AUG_STRIDE = 8
CUM_PARTS = 3
ONES_LANE = ATTN_HEADS

VMEM_LIMIT = 56 * 1024 * 1024

PROJ_ROWS = 512
RNN_ROWS = 256
RNN_BATCH_ROWS = 2
RNN_STAGES = 6
RNN_STAGE_SKEW = 1
ATTN_TK = 256
ATTN_TQ = 4 * ATTN_TK
ATTN_HEAD_BLOCKS_PER_STEP = 2
BF16_SUBLANES = 2 * SUBLANES
ACC_ROWS = ATTN_HEAD_DIM + BF16_SUBLANES


def _softplus(x):
    return jnp.maximum(x, 0.0) + jnp.log1p(jnp.exp(-jnp.abs(x)))


def _log_sigmoid(x):
    return jnp.minimum(x, 0.0) - jnp.log1p(jnp.exp(-jnp.abs(x)))


def _split_bf16(x):
    hi = x.astype(BF16)
    r1 = x - hi.astype(F32)
    mid = r1.astype(BF16)
    lo = (r1 - mid.astype(F32)).astype(BF16)
    return hi, mid, lo


def _deepnorm_ln(x, h, g, b):
    z = DEEPNORM_ALPHA * x + h
    mu = jnp.mean(z, axis=-1, keepdims=True)
    zc = z - mu
    var = jnp.mean(zc * zc, axis=-1, keepdims=True)
    return zc * lax.rsqrt(var + LN_EPS) * g + b


def _dot_nt(a, b):
    return lax.dot_general(a, b, (((1,), (1,)), ((), ())), preferred_element_type=F32)


def _attn_proj_kernel(x_ref, wk_ref, wg_ref, wqt_ref, wvt_ref, wf_ref, bf_ref, tri_ref, eqt_ref, ek_ref,
                      qt_ref, qat_ref, k_ref, ka_ref, vt_ref, g_ref, carry_ref, *, tm, tk):
    si = pl.program_id(1)

    @pl.when(si == 0)
    def _():
        carry_ref[...] = jnp.zeros_like(carry_ref)

    xb = x_ref[0].astype(BF16)
    k_ref[0] = jnp.dot(xb, wk_ref[...], preferred_element_type=F32).astype(BF16)
    g_ref[0] = jnp.dot(xb, wg_ref[...], preferred_element_type=F32).astype(BF16)
    qt_ref[0] = (_dot_nt(wqt_ref[...], xb) * (ATTN_HEAD_DIM ** -0.5 * LOG2_E)).astype(BF16)
    vt = _dot_nt(wvt_ref[...], xb).astype(BF16)
    for hb in range(HEAD_BLOCKS):
        for c in range(tm // tk):
            vt_ref[0, hb, c] = vt[hb * LANES:(hb + 1) * LANES, c * tk:(c + 1) * tk]

    f = jnp.dot(xb, wf_ref[...], preferred_element_type=F32) + bf_ref[...]
    ls = _log_sigmoid(f)
    tri = tri_ref[...]
    cum = carry_ref[0:1, :]
    for part in _split_bf16(ls):
        cum = cum + jnp.dot(tri, part, preferred_element_type=F32)
    carry_ref[0:1, :] = cum[tm - 1:tm, :]

    hi, mid, lo = _split_bf16(cum * LOG2_E)
    lane = lax.broadcasted_iota(jnp.int32, hi.shape, 1)
    hi = jnp.where(lane == ONES_LANE, jnp.ones_like(hi), hi)
    parts = jnp.concatenate([hi, mid, lo], axis=1)
    qat_ref[0] = _dot_nt(eqt_ref[...], parts).astype(BF16)
    ka_ref[0] = jnp.dot(parts, ek_ref[...], preferred_element_type=F32).astype(BF16)


def _bias_expanders():
    assert ATTN_HEADS * AUG_STRIDE == LANES
    eq = np.zeros((CUM_PARTS * LANES, LANES), np.float32)
    ek = np.zeros_like(eq)
    for h in range(ATTN_HEADS):
        base = h * AUG_STRIDE
        for p in range(CUM_PARTS):
            eq[p * LANES + h, base + p] = 1.0
            ek[ONES_LANE, base + p] = 1.0
            ek[p * LANES + h, base + CUM_PARTS + p] = -1.0
            eq[ONES_LANE, base + CUM_PARTS + p] = 1.0
    return jnp.asarray(eq.T, BF16), jnp.asarray(ek, BF16)


def _attn_proj(x, w_k, w_g, w_qt, w_vt, w_f, b_f):
    B, S, D = x.shape
    tm, tk = PROJ_ROWS, ATTN_TK
    tri = jnp.asarray(np.tril(np.ones((tm, tm), np.float32)), BF16)
    eqt, ek = _bias_expanders()
    consts = [w_k, w_g, w_qt, w_vt, w_f, b_f, tri, eqt, ek]
    const = lambda b, s: (0, 0)
    row_blk = lambda b, s: (b, s, 0)
    col_blk = lambda b, s: (b, 0, s)
    s_major = pl.BlockSpec((1, tm, ATTN_WIDTH), row_blk)
    f_major = pl.BlockSpec((1, ATTN_WIDTH, tm), col_blk)
    vt_blk = pl.BlockSpec((1, HEAD_BLOCKS, tm // tk, LANES, tk), lambda b, s: (b, 0, s, 0, 0))
    s_shape = jax.ShapeDtypeStruct((B, S, ATTN_WIDTH), BF16)
    f_shape = jax.ShapeDtypeStruct((B, ATTN_WIDTH, S), BF16)
    vt_shape = jax.ShapeDtypeStruct((B, HEAD_BLOCKS, S // tk, LANES, tk), BF16)
    s_slots = pl.BlockSpec((1, tm, LANES), row_blk)
    f_slots = pl.BlockSpec((1, LANES, tm), col_blk)
    s_slots_shape = jax.ShapeDtypeStruct((B, S, LANES), BF16)
    f_slots_shape = jax.ShapeDtypeStruct((B, LANES, S), BF16)
    return pl.pallas_call(
        functools.partial(_attn_proj_kernel, tm=tm, tk=tk),
        grid=(B, S // tm),
        in_specs=[pl.BlockSpec((1, tm, D), row_blk)] + [pl.BlockSpec(c.shape, const) for c in consts],
        out_specs=[f_major, f_slots, s_major, s_slots, vt_blk, s_major],
        out_shape=[f_shape, f_slots_shape, s_shape, s_slots_shape, vt_shape, s_shape],
        scratch_shapes=[pltpu.VMEM((SUBLANES, LANES), F32)],
        compiler_params=pltpu.CompilerParams(
            dimension_semantics=("arbitrary", "arbitrary"), vmem_limit_bytes=VMEM_LIMIT),
        name="attn_proj",
    )(x, *consts)


def _flash_kernel(qt_ref, qat_ref, k_ref, ka_ref, vt_ref, o_ref, s_scr, p_scr, acc_scr, *, tq, tk, hb):
    i = pl.program_id(2)
    ratio = tq // tk
    n_visits = ratio * i + ratio
    n_heads = hb * HEADS_PER_BLOCK
    qrow = lax.broadcasted_iota(jnp.int32, (2 * LANES, tq), 0)
    ones = jnp.ones((ACC_ROWS - ATTN_HEAD_DIM, tk), BF16)

    first_head = pl.program_id(1) * n_heads
    ka_all = ka_ref.at[0]
    q_heads = []
    for c in range(hb):
        q_all = jnp.concatenate([qt_ref[0, c * LANES:(c + 1) * LANES], qat_ref[0]], axis=0)
        for j in range(HEADS_PER_BLOCK):
            lo = j * ATTN_HEAD_DIM
            slot = LANES + (first_head + c * HEADS_PER_BLOCK + j) * AUG_STRIDE
            mine = ((qrow >= lo) & (qrow < lo + ATTN_HEAD_DIM)) | ((qrow >= slot) & (qrow < slot + AUG_STRIDE))
            q_heads.append(jnp.where(mine, q_all, jnp.zeros_like(q_all)))

    def block_of(visit, diagonal):
        return ratio * i + visit if diagonal else ratio * i + ratio - 1 - visit

    def logits(visit, diagonal=False):
        lo = visit * tk if diagonal else 0
        ks = pl.multiple_of(block_of(visit, diagonal) * tk, tk)
        maxima = []
        ka = ka_all[pl.ds(ks, tk), :]
        for c in range(hb):
            kk = jnp.concatenate([k_ref[0, pl.ds(ks, tk), c * LANES:(c + 1) * LANES], ka], axis=1)
            for j in range(HEADS_PER_BLOCK):
                h = c * HEADS_PER_BLOCK + j
                s = jnp.dot(kk, q_heads[h][:, lo:], preferred_element_type=F32)
                if diagonal:
                    key = lax.broadcasted_iota(jnp.int32, s.shape, 0)
                    qry = lax.broadcasted_iota(jnp.int32, s.shape, 1)
                    s = jnp.where(key <= qry, s, -jnp.inf)
                s_scr[h, :, lo:] = s
                maxima.append(jnp.broadcast_to(jnp.max(s, axis=0, keepdims=True), (SUBLANES, s.shape[1])))
        return tuple(maxima)

    def softmax(m, block_max, lo=0):
        m_out, alpha_out = [], []
        for h in range(n_heads):
            m_old = m[h][:, lo:]
            m_new = jnp.maximum(m_old, block_max[h])
            alpha_out.append(jnp.exp2(m_old - m_new))
            p_scr[h, :, lo:] = jnp.exp2(s_scr[h, :, lo:] - m_new[0:1]).astype(BF16)
            m_out.append(m_new if lo == 0 else jnp.concatenate([m[h][:, :lo], m_new], axis=1))
        return tuple(m_out), tuple(alpha_out)

    def accumulate(visit, alpha, diagonal=False):
        lo = visit * tk if diagonal else 0
        kb = block_of(visit, diagonal)
        for c in range(hb):
            vt = vt_ref[0, c, kb]
            for j in range(HEADS_PER_BLOCK):
                h = c * HEADS_PER_BLOCK + j
                v1 = jnp.concatenate([vt[j * ATTN_HEAD_DIM:(j + 1) * ATTN_HEAD_DIM], ones], axis=0)
                acc_scr[h, :, lo:] = alpha[h][0:1] * acc_scr[h, :, lo:] + jnp.dot(
                    v1, p_scr[h, :, lo:], preferred_element_type=F32)

    acc_scr[...] = jnp.zeros_like(acc_scr)
    m = tuple(jnp.full((SUBLANES, tq), -jnp.inf, F32) for _ in range(n_heads))

    block_max = logits(0, diagonal=True)
    m, alpha = softmax(m, block_max)
    block_max = logits(1, diagonal=True)
    for v in range(1, ratio - 1):
        accumulate(v - 1, alpha, diagonal=True)
        m, alpha = softmax(m, block_max, lo=v * tk)
        block_max = logits(v + 1, diagonal=True)
    last = ratio - 1

    @pl.when(i == 0)
    def _():
        accumulate(last - 1, alpha, diagonal=True)
        _, alpha_last = softmax(m, block_max, lo=last * tk)
        accumulate(last, alpha_last, diagonal=True)

    @pl.when(i > 0)
    def _():
        accumulate(last - 1, alpha, diagonal=True)
        m1, alpha1 = softmax(m, block_max, lo=last * tk)
        block_max1 = logits(ratio)
        accumulate(last, alpha1, diagonal=True)
        m2, alpha2 = softmax(m1, block_max1)
        carry = (m2, alpha2, logits(ratio + 1))

        def step(visit, carry):
            m, alpha_prev, block_max = carry
            accumulate(visit - 1, alpha_prev)
            m, alpha = softmax(m, block_max)
            return m, alpha, logits(visit + 1)

        def step_pair(t, carry):
            visit = ratio + 1 + 2 * t
            return step(visit + 1, step(visit, carry))

        m3, alpha3, block_max3 = lax.fori_loop(0, (ratio * i - 2) // 2, step_pair, carry)
        accumulate(n_visits - 2, alpha3)
        _, alpha_last = softmax(m3, block_max3)
        accumulate(n_visits - 1, alpha_last)

    d = ATTN_HEAD_DIM
    for c in range(hb):
        pair = []
        for j in range(HEADS_PER_BLOCK):
            acc = acc_scr[c * HEADS_PER_BLOCK + j]
            pair.append(acc[:d] / acc[d:d + 1])
        o_ref[0, :, c * LANES:(c + 1) * LANES] = jnp.concatenate(pair, axis=0).T.astype(BF16)


def _flash(qt, qat, k, ka, vt):
    B, S, _ = k.shape
    tq, tk, hb = ATTN_TQ, ATTN_TK, ATTN_HEAD_BLOCKS_PER_STEP
    assert tq % (2 * tk) == 0 and S % tq == 0
    n_heads = hb * HEADS_PER_BLOCK
    q_blk = pl.BlockSpec((1, hb * LANES, tq), lambda b, h, i: (b, h, i))
    q_slots = pl.BlockSpec((1, LANES, tq), lambda b, h, i: (b, 0, i))
    k_blk = pl.BlockSpec((1, S, hb * LANES), lambda b, h, i: (b, 0, h))
    k_slots = pl.BlockSpec((1, S, LANES), lambda b, h, i: (b, 0, 0))
    v_blk = pl.BlockSpec((1, hb, S // tk, LANES, tk), lambda b, h, i: (b, h, 0, 0, 0))
    return pl.pallas_call(
        functools.partial(_flash_kernel, tq=tq, tk=tk, hb=hb),
        grid=(B, HEAD_BLOCKS // hb, S // tq),
        in_specs=[q_blk, q_slots, k_blk, k_slots, v_blk],
        out_specs=pl.BlockSpec((1, tq, hb * LANES), lambda b, h, i: (b, i, h)),
        out_shape=jax.ShapeDtypeStruct((B, S, ATTN_WIDTH), BF16),
        scratch_shapes=[
            pltpu.VMEM((n_heads, tk, tq), F32),
            pltpu.VMEM((n_heads, tk, tq), BF16),
            pltpu.VMEM((n_heads, ACC_ROWS, tq), F32),
        ],
        compiler_params=pltpu.CompilerParams(
            dimension_semantics=("arbitrary", "arbitrary", "arbitrary"), vmem_limit_bytes=VMEM_LIMIT),
        name="fox_flash",
    )(qt, qat, k, ka, vt)


def _chunk_interleave(tm):
    steps = tm // SUBLANES
    p = np.zeros((tm, tm), np.float32)
    for s in range(SUBLANES):
        for r in range(steps):
            p[r * SUBLANES + s, s * steps + r] = 1.0
    return p


def _rnn_kernel(x_ref, ao_ref, ag_ref, *refs, tm, nb):
    consts, (y_ref, tail_ref, h_ref) = refs[:-3], refs[-3:]
    si = pl.program_id(1)

    @pl.when(si == 0)
    def _():
        tail_ref[...] = jnp.zeros_like(tail_ref)
        h_ref[...] = jnp.zeros_like(h_ref)

    tiles = [_rnn_tile(x_ref.at[bb], ao_ref.at[bb], ag_ref.at[bb], *consts,
                       y_ref.at[bb], tail_ref.at[bb], h_ref.at[bb], tm=tm)
             for bb in range(nb)]
    order = sorted(((RNN_STAGE_SKEW * bb + stage, -bb, bb) for bb in range(nb) for stage in range(RNN_STAGES)))
    for _, _, bb in order:
        next(tiles[bb], None)


def _rnn_tile(x_ref, ao_ref, ag_ref, aw_ref, alg_ref, alb_ref, perm_ref, unperm_ref, win_ref, cw_ref, cb_ref,
              wa_ref, ba_ref, wi_ref, bi_ref, lam_ref, wout_ref, lg_ref, lb_ref, y_ref, tail_ref, h_ref, *, tm):
    w = RNN_WIDTH
    steps = tm // SUBLANES
    ag_half = ag_ref[...].astype(F32)
    attn = ao_ref[...].astype(F32) * (ag_half * (jnp.tanh(ag_half) + 1.0))
    attn = jnp.dot(attn.astype(BF16), aw_ref[...], preferred_element_type=F32)
    x = _deepnorm_ln(x_ref[...], attn, alg_ref[...], alb_ref[...])
    yield
    xb = jnp.dot(perm_ref[...], x.astype(BF16), preferred_element_type=F32).astype(BF16)
    u = jnp.dot(xb, win_ref[:, 0:w], preferred_element_type=F32)
    gate_half = jnp.dot(xb, win_ref[:, w:2 * w], preferred_element_type=F32)
    yield

    n_tail = (CONV_WIDTH - 1) * SUBLANES
    row8 = lax.broadcasted_iota(jnp.int32, (SUBLANES, w), 0)
    wrapped = []
    for g in range(CONV_WIDTH - 1):
        lo = tm - n_tail + g * SUBLANES
        here = pltpu.roll(u[lo:lo + SUBLANES], 1, 0)
        before = pltpu.roll(tail_ref[g * SUBLANES:(g + 1) * SUBLANES, :], 1, 0)
        wrapped.append(jnp.where(row8 == 0, before, here))
    conv_half = cb_ref[...] + u * cw_ref[CONV_WIDTH - 1:CONV_WIDTH, :]
    for back in range(1, CONV_WIDTH):
        shifted = jnp.concatenate(wrapped[CONV_WIDTH - 1 - back:] + [u[:tm - back * SUBLANES]], axis=0)
        tap = CONV_WIDTH - 1 - back
        conv_half = conv_half + shifted * cw_ref[tap:tap + 1, :]
    tail_ref[...] = u[tm - n_tail:tm]

    cbf = conv_half.astype(BF16)
    bw = RNN_BLOCK_WIDTH
    ra = jnp.concatenate(
        [jnp.dot(cbf[:, n * bw:(n + 1) * bw], wa_ref[n], preferred_element_type=F32) for n in range(RNN_BLOCKS)],
        axis=1)
    ri = jnp.concatenate(
        [jnp.dot(cbf[:, n * bw:(n + 1) * bw], wi_ref[n], preferred_element_type=F32) for n in range(RNN_BLOCKS)],
        axis=1)
    yield
    tanh_r = jnp.tanh(ra + ba_ref[...])
    tanh_i = jnp.tanh(ri + bi_ref[...])
    c_half = (-0.5 * LRU_C) * _softplus(-lam_ref[...])
    log_a = c_half * tanh_r + c_half
    a = jnp.exp(log_a)
    gain2 = jnp.tanh(-log_a) * (a * a + 1.0)
    gain = gain2 * lax.rsqrt(jnp.maximum(gain2, F32_MIN_NORMAL))
    b = gain * ((tanh_i + 1.0) * conv_half)

    h_loc = jnp.zeros((SUBLANES, w), F32)
    a_cum = jnp.ones((SUBLANES, w), F32)
    h_locs, a_cums = [], []
    for step in range(steps):
        a_step = a[step * SUBLANES:(step + 1) * SUBLANES]
        h_loc = a_step * h_loc + b[step * SUBLANES:(step + 1) * SUBLANES]
        a_cum = a_step * a_cum
        h_locs.append(h_loc)
        a_cums.append(a_cum)
    state = h_ref[0:1, :]
    entering = []
    for s in range(SUBLANES):
        entering.append(state)
        state = a_cum[s:s + 1] * state + h_loc[s:s + 1]
    h_ref[0:1, :] = state
    h_in = jnp.concatenate(entering, axis=0)
    h = jnp.concatenate([h_locs[step] + a_cums[step] * h_in for step in range(steps)], axis=0)

    y = (h * (gate_half * (jnp.tanh(gate_half) + 1.0))).astype(BF16)
    yield
    y = jnp.dot(unperm_ref[...], y, preferred_element_type=F32).astype(BF16)
    out = jnp.dot(y, wout_ref[...], preferred_element_type=F32)
    yield
    y_ref[...] = _deepnorm_ln(x, out, lg_ref[...], lb_ref[...])


def _attn_out_rnn_layer(x, attn_o, attn_gate_half, attn_w_out, attn_ln_g, attn_ln_b,
                        w_in, conv_w, conv_b, w_a, b_a, w_i, b_i, lam, w_out, ln_g, ln_b):
    B, S, D = x.shape
    tm = RNN_ROWS
    row_blk = lambda b, s: (b, s, 0)

    def const(arr):
        nd = arr.ndim
        return pl.BlockSpec(arr.shape, lambda b, s: (0,) * nd)

    perm = _chunk_interleave(tm)
    consts = [attn_w_out, attn_ln_g, attn_ln_b, jnp.asarray(perm, BF16), jnp.asarray(perm.T, BF16),
              w_in, conv_w, conv_b, w_a, b_a, w_i, b_i, lam, w_out, ln_g, ln_b]
    nb = RNN_BATCH_ROWS
    assert B % nb == 0
    rows = [x, attn_o, attn_gate_half]
    return pl.pallas_call(
        functools.partial(_rnn_kernel, tm=tm, nb=nb),
        grid=(B // nb, S // tm),
        in_specs=[pl.BlockSpec((nb, tm, r.shape[-1]), row_blk) for r in rows] + [const(c) for c in consts],
        out_specs=pl.BlockSpec((nb, tm, D), row_blk),
        out_shape=jax.ShapeDtypeStruct((B, S, D), F32),
        scratch_shapes=[pltpu.VMEM((nb, (CONV_WIDTH - 1) * SUBLANES, RNN_WIDTH), F32),
                        pltpu.VMEM((nb, SUBLANES, RNN_WIDTH), F32)],
        compiler_params=pltpu.CompilerParams(
            dimension_semantics=("arbitrary", "arbitrary"), vmem_limit_bytes=VMEM_LIMIT),
        name="rglru_layer",
    )(*rows, *consts)


def _row(v):
    return v.reshape(1, -1).astype(F32)


def kernel(x, ln_g, ln_b, attn_w_in, attn_b_f, attn_w_out, rnn_w_in, rnn_conv_w, rnn_conv_b,
           rnn_w_a, rnn_b_a, rnn_w_i, rnn_b_i, rnn_lambda, rnn_w_out):
    assert x.shape[-1] == D_MODEL and x.shape[1] % ATTN_TQ == 0
    assert N_MIXERS == 2 and DEPTH % N_MIXERS == 0
    w = ATTN_WIDTH
    for layer in range(0, DEPTH, N_MIXERS):
        idx = layer // N_MIXERS
        w_in = attn_w_in[idx].astype(BF16)
        w_f = jnp.pad(w_in[:, 4 * w:], ((0, 0), (0, LANES - ATTN_HEADS)))
        b_f = jnp.pad(attn_b_f[idx].astype(F32), (0, LANES - ATTN_HEADS)).reshape(1, LANES)
        qt, qat, k, ka, vt, gate_half = _attn_proj(
            x, w_in[:, w:2 * w], 0.5 * w_in[:, 3 * w:4 * w], w_in[:, 0:w].T, w_in[:, 2 * w:3 * w].T, w_f, b_f)
        o = _flash(qt, qat, k, ka, vt)
        rnn_in = rnn_w_in[idx] * jnp.where(jnp.arange(2 * RNN_WIDTH) < RNN_WIDTH, 1.0, 0.5)
        x = _attn_out_rnn_layer(
            x, o, gate_half, attn_w_out[idx].astype(BF16), _row(ln_g[layer]), _row(ln_b[layer]),
            rnn_in.astype(BF16), 0.5 * rnn_conv_w[idx].astype(F32), 0.5 * _row(rnn_conv_b[idx]),
            rnn_w_a[idx].astype(BF16), 0.5 * _row(rnn_b_a[idx]),
            rnn_w_i[idx].astype(BF16), 0.5 * _row(rnn_b_i[idx]),
            _row(rnn_lambda[idx]), rnn_w_out[idx].astype(BF16), _row(ln_g[layer + 1]), _row(ln_b[layer + 1]))
    return x
```

```python
import functools

import jax
import jax.numpy as jnp
import numpy as np
from jax import lax
from jax.experimental import pallas as pl
from jax.experimental.pallas import tpu as pltpu

F32 = jnp.float32
BF16 = jnp.bfloat16

D_MODEL = 1024
DEPTH = 4
N_MIXERS = 2
ATTN_HEADS = 16
ATTN_HEAD_DIM = 64
ATTN_WIDTH = ATTN_HEADS * ATTN_HEAD_DIM
RNN_WIDTH = D_MODEL
RNN_BLOCK_WIDTH = 256
RNN_BLOCKS = RNN_WIDTH // RNN_BLOCK_WIDTH
CONV_WIDTH = 4
LRU_C = 8.0
DEEPNORM_ALPHA = (2.0 * DEPTH) ** 0.25
LN_EPS = 1e-5
LOG2_E = 1.4426950408889634
F32_MIN_NORMAL = float(np.finfo(np.float32).tiny)

LANES = 128
SUBLANES = 8
HEADS_PER_BLOCK = LANES // ATTN_HEAD_DIM
HEAD_BLOCKS = ATTN_HEADS // HEADS_PER_BLOCK
AUG_STRIDE = 8
CUM_PARTS = 3
ONES_LANE = ATTN_HEADS

VMEM_LIMIT = 56 * 1024 * 1024

PROJ_ROWS = 512
RNN_ROWS = 256
RNN_BATCH_ROWS = 2
RNN_STAGES = 6
RNN_STAGE_SKEW = 1
ATTN_TK = 256
ATTN_TQ = 4 * ATTN_TK
ATTN_HEAD_BLOCKS_PER_STEP = 2
BF16_SUBLANES = 2 * SUBLANES
ACC_ROWS = ATTN_HEAD_DIM + BF16_SUBLANES


def _softplus(x):
    return jnp.maximum(x, 0.0) + jnp.log1p(jnp.exp(-jnp.abs(x)))


def _log_sigmoid(x):
    return jnp.minimum(x, 0.0) - jnp.log1p(jnp.exp(-jnp.abs(x)))


def _split_bf16(x):
    hi = x.astype(BF16)
    r1 = x - hi.astype(F32)
    mid = r1.astype(BF16)
    lo = (r1 - mid.astype(F32)).astype(BF16)
    return hi, mid, lo


def _deepnorm_ln(x, h, g, b):
    z = DEEPNORM_ALPHA * x + h
    mu = jnp.mean(z, axis=-1, keepdims=True)
    zc = z - mu
    var = jnp.mean(zc * zc, axis=-1, keepdims=True)
    return zc * lax.rsqrt(var + LN_EPS) * g + b


def _dot_nt(a, b):
    return lax.dot_general(a, b, (((1,), (1,)), ((), ())), preferred_element_type=F32)


def _attn_proj_kernel(x_ref, wkf_ref, wg_ref, wqt_ref, wvt_ref, bf_ref, eqt_ref, ek_ref,
                      qt_ref, qat_ref, k_ref, ka_ref, vt_ref, g_ref, carry_ref, *, tm, tk):
    si = pl.program_id(1)

    @pl.when(si == 0)
    def _():
        carry_ref[...] = jnp.zeros_like(carry_ref)

    xb = x_ref[0].astype(BF16)
    kf = jnp.dot(xb, wkf_ref[...], preferred_element_type=F32)
    k_ref[0] = kf[:, :ATTN_WIDTH].astype(BF16)
    g_ref[0] = jnp.dot(xb, wg_ref[...], preferred_element_type=F32).astype(BF16)
    qt_ref[0] = (_dot_nt(wqt_ref[...], xb) * (ATTN_HEAD_DIM ** -0.5 * LOG2_E)).astype(BF16)
    vt = _dot_nt(wvt_ref[...], xb).astype(BF16)
    for hb in range(HEAD_BLOCKS):
        for c in range(tm // tk):
            vt_ref[0, hb, c] = vt[hb * LANES:(hb + 1) * LANES, c * tk:(c + 1) * tk]

    cum = _log_sigmoid(kf[:, ATTN_WIDTH:] + bf_ref[...])
    row = lax.broadcasted_iota(jnp.int32, cum.shape, 0)
    shift = 1
    while shift < tm:
        cum = cum + jnp.where(row >= shift, pltpu.roll(cum, shift, 0), 0.0)
        shift *= 2
    cum = cum + carry_ref[0:1, :]
    carry_ref[0:1, :] = cum[tm - 1:tm, :]

    hi, mid, lo = _split_bf16(cum * LOG2_E)
    lane = lax.broadcasted_iota(jnp.int32, hi.shape, 1)
    hi = jnp.where(lane == ONES_LANE, jnp.ones_like(hi), hi)
    parts = jnp.concatenate([hi, mid, lo], axis=1)
    qat_ref[0] = _dot_nt(eqt_ref[...], parts).astype(BF16)
    ka_ref[0] = jnp.dot(parts, ek_ref[...], preferred_element_type=F32).astype(BF16)


def _bias_expanders():
    assert ATTN_HEADS * AUG_STRIDE == LANES
    eq = np.zeros((CUM_PARTS * LANES, LANES), np.float32)
    ek = np.zeros_like(eq)
    for h in range(ATTN_HEADS):
        base = h * AUG_STRIDE
        for p in range(CUM_PARTS):
            eq[p * LANES + h, base + p] = 1.0
            ek[ONES_LANE, base + p] = 1.0
            ek[p * LANES + h, base + CUM_PARTS + p] = -1.0
            eq[ONES_LANE, base + CUM_PARTS + p] = 1.0
    return jnp.asarray(eq.T, BF16), jnp.asarray(ek, BF16)


def _attn_proj(x, w_kf, w_g, w_qt, w_vt, b_f):
    B, S, D = x.shape
    tm, tk = PROJ_ROWS, ATTN_TK
    eqt, ek = _bias_expanders()
    consts = [w_kf, w_g, w_qt, w_vt, b_f, eqt, ek]
    const = lambda b, s: (0, 0)
    row_blk = lambda b, s: (b, s, 0)
    col_blk = lambda b, s: (b, 0, s)
    s_major = pl.BlockSpec((1, tm, ATTN_WIDTH), row_blk)
    f_major = pl.BlockSpec((1, ATTN_WIDTH, tm), col_blk)
    vt_blk = pl.BlockSpec((1, HEAD_BLOCKS, tm // tk, LANES, tk), lambda b, s: (b, 0, s, 0, 0))
    s_shape = jax.ShapeDtypeStruct((B, S, ATTN_WIDTH), BF16)
    f_shape = jax.ShapeDtypeStruct((B, ATTN_WIDTH, S), BF16)
    vt_shape = jax.ShapeDtypeStruct((B, HEAD_BLOCKS, S // tk, LANES, tk), BF16)
    s_slots = pl.BlockSpec((1, tm, LANES), row_blk)
    f_slots = pl.BlockSpec((1, LANES, tm), col_blk)
    s_slots_shape = jax.ShapeDtypeStruct((B, S, LANES), BF16)
    f_slots_shape = jax.ShapeDtypeStruct((B, LANES, S), BF16)
    return pl.pallas_call(
        functools.partial(_attn_proj_kernel, tm=tm, tk=tk),
        grid=(B, S // tm),
        in_specs=[pl.BlockSpec((1, tm, D), row_blk)] + [pl.BlockSpec(c.shape, const) for c in consts],
        out_specs=[f_major, f_slots, s_major, s_slots, vt_blk, s_major],
        out_shape=[f_shape, f_slots_shape, s_shape, s_slots_shape, vt_shape, s_shape],
        scratch_shapes=[pltpu.VMEM((SUBLANES, LANES), F32)],
        compiler_params=pltpu.CompilerParams(
            dimension_semantics=("arbitrary", "arbitrary"), vmem_limit_bytes=VMEM_LIMIT),
        name="attn_proj",
    )(x, *consts)


def _flash_kernel(qt_ref, qat_ref, k_ref, ka_ref, vt_ref, o_ref, s_scr, p_scr, acc_scr, *, tq, tk, hb):
    i = pl.program_id(2)
    ratio = tq // tk
    n_visits = ratio * i + ratio
    n_heads = hb * HEADS_PER_BLOCK
    qrow = lax.broadcasted_iota(jnp.int32, (2 * LANES, tq), 0)
    ones = jnp.ones((ACC_ROWS - ATTN_HEAD_DIM, tk), BF16)

    first_head = pl.program_id(1) * n_heads
    ka_all = ka_ref.at[0]
    q_heads = []
    for c in range(hb):
        q_all = jnp.concatenate([qt_ref[0, c * LANES:(c + 1) * LANES], qat_ref[0]], axis=0)
        for j in range(HEADS_PER_BLOCK):
            lo = j * ATTN_HEAD_DIM
            slot = LANES + (first_head + c * HEADS_PER_BLOCK + j) * AUG_STRIDE
            mine = ((qrow >= lo) & (qrow < lo + ATTN_HEAD_DIM)) | ((qrow >= slot) & (qrow < slot + AUG_STRIDE))
            q_heads.append(jnp.where(mine, q_all, jnp.zeros_like(q_all)))

    def block_of(visit, diagonal):
        return ratio * i + visit if diagonal else ratio * i + ratio - 1 - visit

    def logits(visit, diagonal=False):
        lo = visit * tk if diagonal else 0
        ks = pl.multiple_of(block_of(visit, diagonal) * tk, tk)
        maxima = []
        ka = ka_all[pl.ds(ks, tk), :]
        for c in range(hb):
            kk = jnp.concatenate([k_ref[0, pl.ds(ks, tk), c * LANES:(c + 1) * LANES], ka], axis=1)
            for j in range(HEADS_PER_BLOCK):
                h = c * HEADS_PER_BLOCK + j
                s = jnp.dot(kk, q_heads[h][:, lo:], preferred_element_type=F32)
                if diagonal:
                    key = lax.broadcasted_iota(jnp.int32, s.shape, 0)
                    qry = lax.broadcasted_iota(jnp.int32, s.shape, 1)
                    s = jnp.where(key <= qry, s, -jnp.inf)
                s_scr[h, :, lo:] = s
                maxima.append(jnp.broadcast_to(jnp.max(s, axis=0, keepdims=True), (SUBLANES, s.shape[1])))
        return tuple(maxima)

    def softmax(m, block_max, lo=0):
        m_out, alpha_out = [], []
        for h in range(n_heads):
            m_old = m[h][:, lo:]
            m_new = jnp.maximum(m_old, block_max[h])
            alpha_out.append(jnp.exp2(m_old - m_new))
            p_scr[h, :, lo:] = jnp.exp2(s_scr[h, :, lo:] - m_new[0:1]).astype(BF16)
            m_out.append(m_new if lo == 0 else jnp.concatenate([m[h][:, :lo], m_new], axis=1))
        return tuple(m_out), tuple(alpha_out)

    def accumulate(visit, alpha, diagonal=False):
        lo = visit * tk if diagonal else 0
        kb = block_of(visit, diagonal)
        for c in range(hb):
            vt = vt_ref[0, c, kb]
            for j in range(HEADS_PER_BLOCK):
                h = c * HEADS_PER_BLOCK + j
                v1 = jnp.concatenate([vt[j * ATTN_HEAD_DIM:(j + 1) * ATTN_HEAD_DIM], ones], axis=0)
                acc_scr[h, :, lo:] = alpha[h][0:1] * acc_scr[h, :, lo:] + jnp.dot(
                    v1, p_scr[h, :, lo:], preferred_element_type=F32)

    acc_scr[...] = jnp.zeros_like(acc_scr)
    m = tuple(jnp.full((SUBLANES, tq), -jnp.inf, F32) for _ in range(n_heads))

    block_max = logits(0, diagonal=True)
    m, alpha = softmax(m, block_max)
    block_max = logits(1, diagonal=True)
    for v in range(1, ratio - 1):
        accumulate(v - 1, alpha, diagonal=True)
        m, alpha = softmax(m, block_max, lo=v * tk)
        block_max = logits(v + 1, diagonal=True)
    last = ratio - 1

    @pl.when(i == 0)
    def _():
        accumulate(last - 1, alpha, diagonal=True)
        _, alpha_last = softmax(m, block_max, lo=last * tk)
        accumulate(last, alpha_last, diagonal=True)

    @pl.when(i > 0)
    def _():
        accumulate(last - 1, alpha, diagonal=True)
        m1, alpha1 = softmax(m, block_max, lo=last * tk)
        block_max1 = logits(ratio)
        accumulate(last, alpha1, diagonal=True)
        m2, alpha2 = softmax(m1, block_max1)
        carry = (m2, alpha2, logits(ratio + 1))

        def step(visit, carry):
            m, alpha_prev, block_max = carry
            accumulate(visit - 1, alpha_prev)
            m, alpha = softmax(m, block_max)
            return m, alpha, logits(visit + 1)

        def step_pair(t, carry):
            visit = ratio + 1 + 2 * t
            return step(visit + 1, step(visit, carry))

        m3, alpha3, block_max3 = lax.fori_loop(0, (ratio * i - 2) // 2, step_pair, carry)
        accumulate(n_visits - 2, alpha3)
        _, alpha_last = softmax(m3, block_max3)
        accumulate(n_visits - 1, alpha_last)

    d = ATTN_HEAD_DIM
    for c in range(hb):
        pair = []
        for j in range(HEADS_PER_BLOCK):
            acc = acc_scr[c * HEADS_PER_BLOCK + j]
            pair.append(acc[:d] / acc[d:d + 1])
        o_ref[0, :, c * LANES:(c + 1) * LANES] = jnp.concatenate(pair, axis=0).T.astype(BF16)


def _flash(qt, qat, k, ka, vt):
    B, S, _ = k.shape
    tq, tk, hb = ATTN_TQ, ATTN_TK, ATTN_HEAD_BLOCKS_PER_STEP
    assert tq % (2 * tk) == 0 and S % tq == 0
    n_heads = hb * HEADS_PER_BLOCK
    q_blk = pl.BlockSpec((1, hb * LANES, tq), lambda b, h, i: (b, h, i))
    q_slots = pl.BlockSpec((1, LANES, tq), lambda b, h, i: (b, 0, i))
    k_blk = pl.BlockSpec((1, S, hb * LANES), lambda b, h, i: (b, 0, h))
    k_slots = pl.BlockSpec((1, S, LANES), lambda b, h, i: (b, 0, 0))
    v_blk = pl.BlockSpec((1, hb, S // tk, LANES, tk), lambda b, h, i: (b, h, 0, 0, 0))
    return pl.pallas_call(
        functools.partial(_flash_kernel, tq=tq, tk=tk, hb=hb),
        grid=(B, HEAD_BLOCKS // hb, S // tq),
        in_specs=[q_blk, q_slots, k_blk, k_slots, v_blk],
        out_specs=pl.BlockSpec((1, tq, hb * LANES), lambda b, h, i: (b, i, h)),
        out_shape=jax.ShapeDtypeStruct((B, S, ATTN_WIDTH), BF16),
        scratch_shapes=[
            pltpu.VMEM((n_heads, tk, tq), F32),
            pltpu.VMEM((n_heads, tk, tq), BF16),
            pltpu.VMEM((n_heads, ACC_ROWS, tq), F32),
        ],
        compiler_params=pltpu.CompilerParams(
            dimension_semantics=("arbitrary", "arbitrary", "arbitrary"), vmem_limit_bytes=VMEM_LIMIT),
        name="fox_flash",
    )(qt, qat, k, ka, vt)


def _chunk_interleave(tm):
    steps = tm // SUBLANES
    p = np.zeros((tm, tm), np.float32)
    for s in range(SUBLANES):
        for r in range(steps):
            p[r * SUBLANES + s, s * steps + r] = 1.0
    return p


def _rnn_kernel(x_ref, ao_ref, ag_ref, *refs, tm, nb):
    consts, (y_ref, tail_ref, h_ref) = refs[:-3], refs[-3:]
    si = pl.program_id(1)

    @pl.when(si == 0)
    def _():
        tail_ref[...] = jnp.zeros_like(tail_ref)
        h_ref[...] = jnp.zeros_like(h_ref)

    tiles = [_rnn_tile(x_ref.at[bb], ao_ref.at[bb], ag_ref.at[bb], *consts,
                       y_ref.at[bb], tail_ref.at[bb], h_ref.at[bb], tm=tm)
             for bb in range(nb)]
    order = sorted(((RNN_STAGE_SKEW * bb + stage, -bb, bb) for bb in range(nb) for stage in range(RNN_STAGES)))
    for _, _, bb in order:
        next(tiles[bb], None)


def _rnn_tile(x_ref, ao_ref, ag_ref, aw_ref, alg_ref, alb_ref, perm_ref, unperm_ref, win_ref, cw_ref, cb_ref,
              wa_ref, ba_ref, wi_ref, bi_ref, lam_ref, wout_ref, lg_ref, lb_ref, y_ref, tail_ref, h_ref, *, tm):
    w = RNN_WIDTH
    steps = tm // SUBLANES
    ag_half = ag_ref[...].astype(F32)
    attn = ao_ref[...].astype(F32) * (ag_half * (jnp.tanh(ag_half) + 1.0))
    attn = jnp.dot(attn.astype(BF16), aw_ref[...], preferred_element_type=F32)
    x = _deepnorm_ln(x_ref[...], attn, alg_ref[...], alb_ref[...])
    yield
    xb = jnp.dot(perm_ref[...], x.astype(BF16), preferred_element_type=F32).astype(BF16)
    u = jnp.dot(xb, win_ref[:, 0:w], preferred_element_type=F32)
    gate_half = jnp.dot(xb, win_ref[:, w:2 * w], preferred_element_type=F32)
    yield

    n_tail = (CONV_WIDTH - 1) * SUBLANES
    row8 = lax.broadcasted_iota(jnp.int32, (SUBLANES, w), 0)
    wrapped = []
    for g in range(CONV_WIDTH - 1):
        lo = tm - n_tail + g * SUBLANES
        here = pltpu.roll(u[lo:lo + SUBLANES], 1, 0)
        before = pltpu.roll(tail_ref[g * SUBLANES:(g + 1) * SUBLANES, :], 1, 0)
        wrapped.append(jnp.where(row8 == 0, before, here))
    conv_half = cb_ref[...] + u * cw_ref[CONV_WIDTH - 1:CONV_WIDTH, :]
    for back in range(1, CONV_WIDTH):
        shifted = jnp.concatenate(wrapped[CONV_WIDTH - 1 - back:] + [u[:tm - back * SUBLANES]], axis=0)
        tap = CONV_WIDTH - 1 - back
        conv_half = conv_half + shifted * cw_ref[tap:tap + 1, :]
    tail_ref[...] = u[tm - n_tail:tm]

    cbf = conv_half.astype(BF16)
    bw = RNN_BLOCK_WIDTH
    ra = jnp.concatenate(
        [jnp.dot(cbf[:, n * bw:(n + 1) * bw], wa_ref[n], preferred_element_type=F32) for n in range(RNN_BLOCKS)],
        axis=1)
    ri = jnp.concatenate(
        [jnp.dot(cbf[:, n * bw:(n + 1) * bw], wi_ref[n], preferred_element_type=F32) for n in range(RNN_BLOCKS)],
        axis=1)
    yield
    tanh_r = jnp.tanh(ra + ba_ref[...])
    tanh_i = jnp.tanh(ri + bi_ref[...])
    c_half = (-0.5 * LRU_C) * _softplus(-lam_ref[...])
    log_a = c_half * tanh_r + c_half
    a = jnp.exp(log_a)
    gain2 = jnp.tanh(-log_a) * (a * a + 1.0)
    gain = gain2 * lax.rsqrt(jnp.maximum(gain2, F32_MIN_NORMAL))
    b = gain * ((tanh_i + 1.0) * conv_half)

    h_loc = jnp.zeros((SUBLANES, w), F32)
    a_cum = jnp.ones((SUBLANES, w), F32)
    h_locs, a_cums = [], []
    for step in range(steps):
        a_step = a[step * SUBLANES:(step + 1) * SUBLANES]
        h_loc = a_step * h_loc + b[step * SUBLANES:(step + 1) * SUBLANES]
        a_cum = a_step * a_cum
        h_locs.append(h_loc)
        a_cums.append(a_cum)
    state = h_ref[0:1, :]
    entering = []
    for s in range(SUBLANES):
        entering.append(state)
        state = a_cum[s:s + 1] * state + h_loc[s:s + 1]
    h_ref[0:1, :] = state
    h_in = jnp.concatenate(entering, axis=0)
    h = jnp.concatenate([h_locs[step] + a_cums[step] * h_in for step in range(steps)], axis=0)

    y = (h * (gate_half * (jnp.tanh(gate_half) + 1.0))).astype(BF16)
    yield
    y = jnp.dot(unperm_ref[...], y, preferred_element_type=F32).astype(BF16)
    out = jnp.dot(y, wout_ref[...], preferred_element_type=F32)
    yield
    y_ref[...] = _deepnorm_ln(x, out, lg_ref[...], lb_ref[...])


def _attn_out_rnn_layer(x, attn_o, attn_gate_half, attn_w_out, attn_ln_g, attn_ln_b,
                        w_in, conv_w, conv_b, w_a, b_a, w_i, b_i, lam, w_out, ln_g, ln_b):
    B, S, D = x.shape
    tm = RNN_ROWS
    row_blk = lambda b, s: (b, s, 0)

    def const(arr):
        nd = arr.ndim
        return pl.BlockSpec(arr.shape, lambda b, s: (0,) * nd)

    perm = _chunk_interleave(tm)
    consts = [attn_w_out, attn_ln_g, attn_ln_b, jnp.asarray(perm, BF16), jnp.asarray(perm.T, BF16),
              w_in, conv_w, conv_b, w_a, b_a, w_i, b_i, lam, w_out, ln_g, ln_b]
    nb = RNN_BATCH_ROWS
    assert B % nb == 0
    rows = [x, attn_o, attn_gate_half]
    return pl.pallas_call(
        functools.partial(_rnn_kernel, tm=tm, nb=nb),
        grid=(B // nb, S // tm),
        in_specs=[pl.BlockSpec((nb, tm, r.shape[-1]), row_blk) for r in rows] + [const(c) for c in consts],
        out_specs=pl.BlockSpec((nb, tm, D), row_blk),
        out_shape=jax.ShapeDtypeStruct((B, S, D), F32),
        scratch_shapes=[pltpu.VMEM((nb, (CONV_WIDTH - 1) * SUBLANES, RNN_WIDTH), F32),
                        pltpu.VMEM((nb, SUBLANES, RNN_WIDTH), F32)],
        compiler_params=pltpu.CompilerParams(
            dimension_semantics=("arbitrary", "arbitrary"), vmem_limit_bytes=VMEM_LIMIT),
        name="rglru_layer",
    )(*rows, *consts)


def _row(v):
    return v.reshape(1, -1).astype(F32)


def kernel(x, ln_g, ln_b, attn_w_in, attn_b_f, attn_w_out, rnn_w_in, rnn_conv_w, rnn_conv_b,
           rnn_w_a, rnn_b_a, rnn_w_i, rnn_b_i, rnn_lambda, rnn_w_out):
    assert x.shape[-1] == D_MODEL and x.shape[1] % ATTN_TQ == 0
    assert N_MIXERS == 2 and DEPTH % N_MIXERS == 0
    w = ATTN_WIDTH
    for layer in range(0, DEPTH, N_MIXERS):
        idx = layer // N_MIXERS
        w_in = attn_w_in[idx].astype(BF16)
        w_kf = jnp.pad(jnp.concatenate([w_in[:, w:2 * w], w_in[:, 4 * w:]], axis=1),
                       ((0, 0), (0, LANES - ATTN_HEADS)))
        b_f = jnp.pad(attn_b_f[idx].astype(F32), (0, LANES - ATTN_HEADS)).reshape(1, LANES)
        qt, qat, k, ka, vt, gate_half = _attn_proj(
            x, w_kf, 0.5 * w_in[:, 3 * w:4 * w], w_in[:, 0:w].T, w_in[:, 2 * w:3 * w].T, b_f)
        o = _flash(qt, qat, k, ka, vt)
        rnn_in = rnn_w_in[idx] * jnp.where(jnp.arange(2 * RNN_WIDTH) < RNN_WIDTH, 1.0, 0.5)
        x = _attn_out_rnn_layer(
            x, o, gate_half, attn_w_out[idx].astype(BF16), _row(ln_g[layer]), _row(ln_b[layer]),
            rnn_in.astype(BF16), 0.5 * rnn_conv_w[idx].astype(F32), 0.5 * _row(rnn_conv_b[idx]),
            rnn_w_a[idx].astype(BF16), 0.5 * _row(rnn_b_a[idx]),
            rnn_w_i[idx].astype(BF16), 0.5 * _row(rnn_b_i[idx]),
            _row(rnn_lambda[idx]), rnn_w_out[idx].astype(BF16), _row(ln_g[layer + 1]), _row(ln_b[layer + 1]))
    return x
```

```python
import functools

import jax
import jax.numpy as jnp
import numpy as np
from jax import lax
from jax.experimental import pallas as pl
from jax.experimental.pallas import tpu as pltpu

F32 = jnp.float32
BF16 = jnp.bfloat16

D_MODEL = 1024
DEPTH = 4
N_MIXERS = 2
ATTN_HEADS = 16
ATTN_HEAD_DIM = 64
ATTN_WIDTH = ATTN_HEADS * ATTN_HEAD_DIM
RNN_WIDTH = D_MODEL
RNN_BLOCK_WIDTH = 256
RNN_BLOCKS = RNN_WIDTH // RNN_BLOCK_WIDTH
CONV_WIDTH = 4
LRU_C = 8.0
DEEPNORM_ALPHA = (2.0 * DEPTH) ** 0.25
LN_EPS = 1e-5
LOG2_E = 1.4426950408889634
F32_MIN_NORMAL = float(np.finfo(np.float32).tiny)

LANES = 128
SUBLANES = 8
HEADS_PER_BLOCK = LANES // ATTN_HEAD_DIM
HEAD_BLOCKS = ATTN_HEADS // HEADS_PER_BLOCK
AUG_STRIDE = 8
CUM_PARTS = 3
ONES_LANE = ATTN_HEADS

VMEM_LIMIT = 56 * 1024 * 1024

PROJ_ROWS = 512
RNN_ROWS = 256
RNN_BATCH_ROWS = 2
RNN_STAGES = 6
RNN_STAGE_SKEW = 1
ATTN_TK = 256
ATTN_TQ = 4 * ATTN_TK
ATTN_HEAD_BLOCKS_PER_STEP = 2
BF16_SUBLANES = 2 * SUBLANES
ACC_ROWS = ATTN_HEAD_DIM + BF16_SUBLANES


def _softplus(x):
    return jnp.maximum(x, 0.0) + jnp.log1p(jnp.exp(-jnp.abs(x)))


def _log_sigmoid(x):
    return jnp.minimum(x, 0.0) - jnp.log1p(jnp.exp(-jnp.abs(x)))


def _split_bf16(x):
    hi = x.astype(BF16)
    r1 = x - hi.astype(F32)
    mid = r1.astype(BF16)
    lo = (r1 - mid.astype(F32)).astype(BF16)
    return hi, mid, lo


def _deepnorm_ln(x, h, g, b):
    z = DEEPNORM_ALPHA * x + h
    mu = jnp.mean(z, axis=-1, keepdims=True)
    zc = z - mu
    var = jnp.mean(zc * zc, axis=-1, keepdims=True)
    return zc * lax.rsqrt(var + LN_EPS) * g + b


def _dot_nt(a, b):
    return lax.dot_general(a, b, (((1,), (1,)), ((), ())), preferred_element_type=F32)


def _attn_proj_kernel(x_ref, wkf_ref, wg_ref, wqt_ref, wvt_ref, bf_ref, eqt_ref, ek_ref,
                      qt_ref, qat_ref, k_ref, ka_ref, vt_ref, g_ref, carry_ref, *, tm, tk):
    si = pl.program_id(1)

    @pl.when(si == 0)
    def _():
        carry_ref[...] = jnp.zeros_like(carry_ref)

    xb = x_ref[0].astype(BF16)
    kf = jnp.dot(xb, wkf_ref[...], preferred_element_type=F32)
    k = kf[:, :ATTN_WIDTH].astype(BF16)
    group = ATTN_HEAD_BLOCKS_PER_STEP * LANES
    for g in range(ATTN_WIDTH // group):
        k_ref[0, g] = k[:, g * group:(g + 1) * group]
    g_ref[0] = jnp.dot(xb, wg_ref[...], preferred_element_type=F32).astype(BF16)
    qt_ref[0] = (_dot_nt(wqt_ref[...], xb) * (ATTN_HEAD_DIM ** -0.5 * LOG2_E)).astype(BF16)
    vt = _dot_nt(wvt_ref[...], xb).astype(BF16)
    for hb in range(HEAD_BLOCKS):
        for c in range(tm // tk):
            vt_ref[0, hb, c] = vt[hb * LANES:(hb + 1) * LANES, c * tk:(c + 1) * tk]

    cum = _log_sigmoid(kf[:, ATTN_WIDTH:] + bf_ref[...])
    row = lax.broadcasted_iota(jnp.int32, cum.shape, 0)
    shift = 1
    while shift < tm:
        cum = cum + jnp.where(row >= shift, pltpu.roll(cum, shift, 0), 0.0)
        shift *= 2
    cum = cum + carry_ref[0:1, :]
    carry_ref[0:1, :] = cum[tm - 1:tm, :]

    hi, mid, lo = _split_bf16(cum * LOG2_E)
    lane = lax.broadcasted_iota(jnp.int32, hi.shape, 1)
    hi = jnp.where(lane == ONES_LANE, jnp.ones_like(hi), hi)
    parts = jnp.concatenate([hi, mid, lo], axis=1)
    qat_ref[0] = _dot_nt(eqt_ref[...], parts).astype(BF16)
    ka_ref[0] = jnp.dot(parts, ek_ref[...], preferred_element_type=F32).astype(BF16)


def _bias_expanders():
    assert ATTN_HEADS * AUG_STRIDE == LANES
    eq = np.zeros((CUM_PARTS * LANES, LANES), np.float32)
    ek = np.zeros_like(eq)
    for h in range(ATTN_HEADS):
        base = h * AUG_STRIDE
        for p in range(CUM_PARTS):
            eq[p * LANES + h, base + p] = 1.0
            ek[ONES_LANE, base + p] = 1.0
            ek[p * LANES + h, base + CUM_PARTS + p] = -1.0
            eq[ONES_LANE, base + CUM_PARTS + p] = 1.0
    return jnp.asarray(eq.T, BF16), jnp.asarray(ek, BF16)


def _attn_proj(x, w_kf, w_g, w_qt, w_vt, b_f):
    B, S, D = x.shape
    tm, tk = PROJ_ROWS, ATTN_TK
    eqt, ek = _bias_expanders()
    consts = [w_kf, w_g, w_qt, w_vt, b_f, eqt, ek]
    const = lambda b, s: (0, 0)
    row_blk = lambda b, s: (b, s, 0)
    col_blk = lambda b, s: (b, 0, s)
    s_major = pl.BlockSpec((1, tm, ATTN_WIDTH), row_blk)
    f_major = pl.BlockSpec((1, ATTN_WIDTH, tm), col_blk)
    vt_blk = pl.BlockSpec((1, HEAD_BLOCKS, tm // tk, LANES, tk), lambda b, s: (b, 0, s, 0, 0))
    s_shape = jax.ShapeDtypeStruct((B, S, ATTN_WIDTH), BF16)
    f_shape = jax.ShapeDtypeStruct((B, ATTN_WIDTH, S), BF16)
    vt_shape = jax.ShapeDtypeStruct((B, HEAD_BLOCKS, S // tk, LANES, tk), BF16)
    group = ATTN_HEAD_BLOCKS_PER_STEP * LANES
    k_groups = pl.BlockSpec((1, ATTN_WIDTH // group, tm, group), lambda b, s: (b, 0, s, 0))
    k_groups_shape = jax.ShapeDtypeStruct((B, ATTN_WIDTH // group, S, group), BF16)
    s_slots = pl.BlockSpec((1, tm, LANES), row_blk)
    f_slots = pl.BlockSpec((1, LANES, tm), col_blk)
    s_slots_shape = jax.ShapeDtypeStruct((B, S, LANES), BF16)
    f_slots_shape = jax.ShapeDtypeStruct((B, LANES, S), BF16)
    return pl.pallas_call(
        functools.partial(_attn_proj_kernel, tm=tm, tk=tk),
        grid=(B, S // tm),
        in_specs=[pl.BlockSpec((1, tm, D), row_blk)] + [pl.BlockSpec(c.shape, const) for c in consts],
        out_specs=[f_major, f_slots, k_groups, s_slots, vt_blk, s_major],
        out_shape=[f_shape, f_slots_shape, k_groups_shape, s_slots_shape, vt_shape, s_shape],
        scratch_shapes=[pltpu.VMEM((SUBLANES, LANES), F32)],
        compiler_params=pltpu.CompilerParams(
            dimension_semantics=("arbitrary", "arbitrary"), vmem_limit_bytes=VMEM_LIMIT),
        name="attn_proj",
    )(x, *consts)


def _flash_kernel(qt_ref, qat_ref, k_ref, ka_ref, vt_ref, o_ref, s_scr, p_scr, acc_scr, *, tq, tk, hb):
    i = pl.program_id(2)
    ratio = tq // tk
    n_visits = ratio * i + ratio
    n_heads = hb * HEADS_PER_BLOCK
    qrow = lax.broadcasted_iota(jnp.int32, (2 * LANES, tq), 0)
    ones = jnp.ones((ACC_ROWS - ATTN_HEAD_DIM, tk), BF16)

    first_head = pl.program_id(1) * n_heads
    ka_all = ka_ref.at[0]
    q_heads = []
    for c in range(hb):
        q_all = jnp.concatenate([qt_ref[0, c * LANES:(c + 1) * LANES], qat_ref[0]], axis=0)
        for j in range(HEADS_PER_BLOCK):
            lo = j * ATTN_HEAD_DIM
            slot = LANES + (first_head + c * HEADS_PER_BLOCK + j) * AUG_STRIDE
            mine = ((qrow >= lo) & (qrow < lo + ATTN_HEAD_DIM)) | ((qrow >= slot) & (qrow < slot + AUG_STRIDE))
            q_heads.append(jnp.where(mine, q_all, jnp.zeros_like(q_all)))

    def block_of(visit, diagonal):
        return ratio * i + visit if diagonal else ratio * i + ratio - 1 - visit

    def logits(visit, diagonal=False):
        lo = visit * tk if diagonal else 0
        ks = pl.multiple_of(block_of(visit, diagonal) * tk, tk)
        maxima = []
        ka = ka_all[pl.ds(ks, tk), :]
        for c in range(hb):
            kk = jnp.concatenate([k_ref[0, 0, pl.ds(ks, tk), c * LANES:(c + 1) * LANES], ka], axis=1)
            for j in range(HEADS_PER_BLOCK):
                h = c * HEADS_PER_BLOCK + j
                s = jnp.dot(kk, q_heads[h][:, lo:], preferred_element_type=F32)
                if diagonal:
                    key = lax.broadcasted_iota(jnp.int32, s.shape, 0)
                    qry = lax.broadcasted_iota(jnp.int32, s.shape, 1)
                    s = jnp.where(key <= qry, s, -jnp.inf)
                s_scr[h, :, lo:] = s
                maxima.append(jnp.broadcast_to(jnp.max(s, axis=0, keepdims=True), (SUBLANES, s.shape[1])))
        return tuple(maxima)

    def softmax(m, block_max, lo=0):
        m_out, alpha_out = [], []
        for h in range(n_heads):
            m_old = m[h][:, lo:]
            m_new = jnp.maximum(m_old, block_max[h])
            alpha_out.append(jnp.exp2(m_old - m_new))
            p_scr[h, :, lo:] = jnp.exp2(s_scr[h, :, lo:] - m_new[0:1]).astype(BF16)
            m_out.append(m_new if lo == 0 else jnp.concatenate([m[h][:, :lo], m_new], axis=1))
        return tuple(m_out), tuple(alpha_out)

    def accumulate(visit, alpha, diagonal=False):
        lo = visit * tk if diagonal else 0
        kb = block_of(visit, diagonal)
        for c in range(hb):
            vt = vt_ref[0, c, kb]
            for j in range(HEADS_PER_BLOCK):
                h = c * HEADS_PER_BLOCK + j
                v1 = jnp.concatenate([vt[j * ATTN_HEAD_DIM:(j + 1) * ATTN_HEAD_DIM], ones], axis=0)
                acc_scr[h, :, lo:] = alpha[h][0:1] * acc_scr[h, :, lo:] + jnp.dot(
                    v1, p_scr[h, :, lo:], preferred_element_type=F32)

    acc_scr[...] = jnp.zeros_like(acc_scr)
    m = tuple(jnp.full((SUBLANES, tq), -jnp.inf, F32) for _ in range(n_heads))

    block_max = logits(0, diagonal=True)
    m, alpha = softmax(m, block_max)
    block_max = logits(1, diagonal=True)
    for v in range(1, ratio - 1):
        accumulate(v - 1, alpha, diagonal=True)
        m, alpha = softmax(m, block_max, lo=v * tk)
        block_max = logits(v + 1, diagonal=True)
    last = ratio - 1

    @pl.when(i == 0)
    def _():
        accumulate(last - 1, alpha, diagonal=True)
        _, alpha_last = softmax(m, block_max, lo=last * tk)
        accumulate(last, alpha_last, diagonal=True)

    @pl.when(i > 0)
    def _():
        accumulate(last - 1, alpha, diagonal=True)
        m1, alpha1 = softmax(m, block_max, lo=last * tk)
        block_max1 = logits(ratio)
        accumulate(last, alpha1, diagonal=True)
        m2, alpha2 = softmax(m1, block_max1)
        carry = (m2, alpha2, logits(ratio + 1))

        def step(visit, carry):
            m, alpha_prev, block_max = carry
            accumulate(visit - 1, alpha_prev)
            m, alpha = softmax(m, block_max)
            return m, alpha, logits(visit + 1)

        def step_pair(t, carry):
            visit = ratio + 1 + 2 * t
            return step(visit + 1, step(visit, carry))

        m3, alpha3, block_max3 = lax.fori_loop(0, (ratio * i - 2) // 2, step_pair, carry)
        accumulate(n_visits - 2, alpha3)
        _, alpha_last = softmax(m3, block_max3)
        accumulate(n_visits - 1, alpha_last)

    d = ATTN_HEAD_DIM
    for c in range(hb):
        pair = []
        for j in range(HEADS_PER_BLOCK):
            acc = acc_scr[c * HEADS_PER_BLOCK + j]
            pair.append(acc[:d] / acc[d:d + 1])
        o_ref[0, :, c * LANES:(c + 1) * LANES] = jnp.concatenate(pair, axis=0).T.astype(BF16)


def _flash(qt, qat, k, ka, vt):
    B, S, _ = ka.shape
    tq, tk, hb = ATTN_TQ, ATTN_TK, ATTN_HEAD_BLOCKS_PER_STEP
    assert tq % (2 * tk) == 0 and S % tq == 0
    n_heads = hb * HEADS_PER_BLOCK
    q_blk = pl.BlockSpec((1, hb * LANES, tq), lambda b, h, i: (b, h, i))
    q_slots = pl.BlockSpec((1, LANES, tq), lambda b, h, i: (b, 0, i))
    k_blk = pl.BlockSpec((1, 1, S, hb * LANES), lambda b, h, i: (b, h, 0, 0))
    k_slots = pl.BlockSpec((1, S, LANES), lambda b, h, i: (b, 0, 0))
    v_blk = pl.BlockSpec((1, hb, S // tk, LANES, tk), lambda b, h, i: (b, h, 0, 0, 0))
    return pl.pallas_call(
        functools.partial(_flash_kernel, tq=tq, tk=tk, hb=hb),
        grid=(B, HEAD_BLOCKS // hb, S // tq),
        in_specs=[q_blk, q_slots, k_blk, k_slots, v_blk],
        out_specs=pl.BlockSpec((1, tq, hb * LANES), lambda b, h, i: (b, i, h)),
        out_shape=jax.ShapeDtypeStruct((B, S, ATTN_WIDTH), BF16),
        scratch_shapes=[
            pltpu.VMEM((n_heads, tk, tq), F32),
            pltpu.VMEM((n_heads, tk, tq), BF16),
            pltpu.VMEM((n_heads, ACC_ROWS, tq), F32),
        ],
        compiler_params=pltpu.CompilerParams(
            dimension_semantics=("arbitrary", "arbitrary", "arbitrary"), vmem_limit_bytes=VMEM_LIMIT),
        name="fox_flash",
    )(qt, qat, k, ka, vt)


def _chunk_interleave(tm):
    steps = tm // SUBLANES
    p = np.zeros((tm, tm), np.float32)
    for s in range(SUBLANES):
        for r in range(steps):
            p[r * SUBLANES + s, s * steps + r] = 1.0
    return p


def _rnn_kernel(x_ref, ao_ref, ag_ref, *refs, tm, nb):
    consts, (y_ref, tail_ref, h_ref) = refs[:-3], refs[-3:]
    si = pl.program_id(1)

    @pl.when(si == 0)
    def _():
        tail_ref[...] = jnp.zeros_like(tail_ref)
        h_ref[...] = jnp.zeros_like(h_ref)

    tiles = [_rnn_tile(x_ref.at[bb], ao_ref.at[bb], ag_ref.at[bb], *consts,
                       y_ref.at[bb], tail_ref.at[bb], h_ref.at[bb], tm=tm)
             for bb in range(nb)]
    order = sorted(((RNN_STAGE_SKEW * bb + stage, -bb, bb) for bb in range(nb) for stage in range(RNN_STAGES)))
    for _, _, bb in order:
        next(tiles[bb], None)


def _rnn_tile(x_ref, ao_ref, ag_ref, aw_ref, alg_ref, alb_ref, perm_ref, unperm_ref, win_ref, cw_ref, cb_ref,
              wa_ref, ba_ref, wi_ref, bi_ref, lam_ref, wout_ref, lg_ref, lb_ref, y_ref, tail_ref, h_ref, *, tm):
    w = RNN_WIDTH
    steps = tm // SUBLANES
    ag_half = ag_ref[...].astype(F32)
    attn = ao_ref[...].astype(F32) * (ag_half * (jnp.tanh(ag_half) + 1.0))
    attn = jnp.dot(attn.astype(BF16), aw_ref[...], preferred_element_type=F32)
    x = _deepnorm_ln(x_ref[...], attn, alg_ref[...], alb_ref[...])
    yield
    xb = jnp.dot(perm_ref[...], x.astype(BF16), preferred_element_type=F32).astype(BF16)
    u = jnp.dot(xb, win_ref[:, 0:w], preferred_element_type=F32)
    gate_half = jnp.dot(xb, win_ref[:, w:2 * w], preferred_element_type=F32)
    yield

    n_tail = (CONV_WIDTH - 1) * SUBLANES
    row8 = lax.broadcasted_iota(jnp.int32, (SUBLANES, w), 0)
    wrapped = []
    for g in range(CONV_WIDTH - 1):
        lo = tm - n_tail + g * SUBLANES
        here = pltpu.roll(u[lo:lo + SUBLANES], 1, 0)
        before = pltpu.roll(tail_ref[g * SUBLANES:(g + 1) * SUBLANES, :], 1, 0)
        wrapped.append(jnp.where(row8 == 0, before, here))
    conv_half = cb_ref[...] + u * cw_ref[CONV_WIDTH - 1:CONV_WIDTH, :]
    for back in range(1, CONV_WIDTH):
        shifted = jnp.concatenate(wrapped[CONV_WIDTH - 1 - back:] + [u[:tm - back * SUBLANES]], axis=0)
        tap = CONV_WIDTH - 1 - back
        conv_half = conv_half + shifted * cw_ref[tap:tap + 1, :]
    tail_ref[...] = u[tm - n_tail:tm]

    cbf = conv_half.astype(BF16)
    bw = RNN_BLOCK_WIDTH
    ra = jnp.concatenate(
        [jnp.dot(cbf[:, n * bw:(n + 1) * bw], wa_ref[n], preferred_element_type=F32) for n in range(RNN_BLOCKS)],
        axis=1)
    ri = jnp.concatenate(
        [jnp.dot(cbf[:, n * bw:(n + 1) * bw], wi_ref[n], preferred_element_type=F32) for n in range(RNN_BLOCKS)],
        axis=1)
    yield
    tanh_r = jnp.tanh(ra + ba_ref[...])
    tanh_i = jnp.tanh(ri + bi_ref[...])
    c_half = (-0.5 * LRU_C) * _softplus(-lam_ref[...])
    log_a = c_half * tanh_r + c_half
    a = jnp.exp(log_a)
    gain2 = jnp.tanh(-log_a) * (a * a + 1.0)
    gain = gain2 * lax.rsqrt(jnp.maximum(gain2, F32_MIN_NORMAL))
    b = gain * ((tanh_i + 1.0) * conv_half)

    h_loc = jnp.zeros((SUBLANES, w), F32)
    a_cum = jnp.ones((SUBLANES, w), F32)
    h_locs, a_cums = [], []
    for step in range(steps):
        a_step = a[step * SUBLANES:(step + 1) * SUBLANES]
        h_loc = a_step * h_loc + b[step * SUBLANES:(step + 1) * SUBLANES]
        a_cum = a_step * a_cum
        h_locs.append(h_loc)
        a_cums.append(a_cum)
    state = h_ref[0:1, :]
    entering = []
    for s in range(SUBLANES):
        entering.append(state)
        state = a_cum[s:s + 1] * state + h_loc[s:s + 1]
    h_ref[0:1, :] = state
    h_in = jnp.concatenate(entering, axis=0)
    h = jnp.concatenate([h_locs[step] + a_cums[step] * h_in for step in range(steps)], axis=0)

    y = (h * (gate_half * (jnp.tanh(gate_half) + 1.0))).astype(BF16)
    yield
    y = jnp.dot(unperm_ref[...], y, preferred_element_type=F32).astype(BF16)
    out = jnp.dot(y, wout_ref[...], preferred_element_type=F32)
    yield
    y_ref[...] = _deepnorm_ln(x, out, lg_ref[...], lb_ref[...])


def _attn_out_rnn_layer(x, attn_o, attn_gate_half, attn_w_out, attn_ln_g, attn_ln_b,
                        w_in, conv_w, conv_b, w_a, b_a, w_i, b_i, lam, w_out, ln_g, ln_b):
    B, S, D = x.shape
    tm = RNN_ROWS
    row_blk = lambda b, s: (b, s, 0)

    def const(arr):
        nd = arr.ndim
        return pl.BlockSpec(arr.shape, lambda b, s: (0,) * nd)

    perm = _chunk_interleave(tm)
    consts = [attn_w_out, attn_ln_g, attn_ln_b, jnp.asarray(perm, BF16), jnp.asarray(perm.T, BF16),
              w_in, conv_w, conv_b, w_a, b_a, w_i, b_i, lam, w_out, ln_g, ln_b]
    nb = RNN_BATCH_ROWS
    assert B % nb == 0
    rows = [x, attn_o, attn_gate_half]
    return pl.pallas_call(
        functools.partial(_rnn_kernel, tm=tm, nb=nb),
        grid=(B // nb, S // tm),
        in_specs=[pl.BlockSpec((nb, tm, r.shape[-1]), row_blk) for r in rows] + [const(c) for c in consts],
        out_specs=pl.BlockSpec((nb, tm, D), row_blk),
        out_shape=jax.ShapeDtypeStruct((B, S, D), F32),
        scratch_shapes=[pltpu.VMEM((nb, (CONV_WIDTH - 1) * SUBLANES, RNN_WIDTH), F32),
                        pltpu.VMEM((nb, SUBLANES, RNN_WIDTH), F32)],
        compiler_params=pltpu.CompilerParams(
            dimension_semantics=("arbitrary", "arbitrary"), vmem_limit_bytes=VMEM_LIMIT),
        name="rglru_layer",
    )(*rows, *consts)


def _row(v):
    return v.reshape(1, -1).astype(F32)


def kernel(x, ln_g, ln_b, attn_w_in, attn_b_f, attn_w_out, rnn_w_in, rnn_conv_w, rnn_conv_b,
           rnn_w_a, rnn_b_a, rnn_w_i, rnn_b_i, rnn_lambda, rnn_w_out):
    assert x.shape[-1] == D_MODEL and x.shape[1] % ATTN_TQ == 0
    assert N_MIXERS == 2 and DEPTH % N_MIXERS == 0
    w = ATTN_WIDTH
    for layer in range(0, DEPTH, N_MIXERS):
        idx = layer // N_MIXERS
        w_in = attn_w_in[idx].astype(BF16)
        w_kf = jnp.pad(jnp.concatenate([w_in[:, w:2 * w], w_in[:, 4 * w:]], axis=1),
                       ((0, 0), (0, LANES - ATTN_HEADS)))
        b_f = jnp.pad(attn_b_f[idx].astype(F32), (0, LANES - ATTN_HEADS)).reshape(1, LANES)
        qt, qat, k, ka, vt, gate_half = _attn_proj(
            x, w_kf, 0.5 * w_in[:, 3 * w:4 * w], w_in[:, 0:w].T, w_in[:, 2 * w:3 * w].T, b_f)
        o = _flash(qt, qat, k, ka, vt)
        rnn_in = rnn_w_in[idx] * jnp.where(jnp.arange(2 * RNN_WIDTH) < RNN_WIDTH, 1.0, 0.5)
        x = _attn_out_rnn_layer(
            x, o, gate_half, attn_w_out[idx].astype(BF16), _row(ln_g[layer]), _row(ln_b[layer]),
            rnn_in.astype(BF16), 0.5 * rnn_conv_w[idx].astype(F32), 0.5 * _row(rnn_conv_b[idx]),
            rnn_w_a[idx].astype(BF16), 0.5 * _row(rnn_b_a[idx]),
            rnn_w_i[idx].astype(BF16), 0.5 * _row(rnn_b_i[idx]),
            _row(rnn_lambda[idx]), rnn_w_out[idx].astype(BF16), _row(ln_g[layer + 1]), _row(ln_b[layer + 1]))
    return x
```

```python
import functools

import jax
import jax.numpy as jnp
import numpy as np
from jax import lax
from jax.experimental import pallas as pl
from jax.experimental.pallas import tpu as pltpu

F32 = jnp.float32
BF16 = jnp.bfloat16

D_MODEL = 1024
DEPTH = 4
N_MIXERS = 2
ATTN_HEADS = 16
ATTN_HEAD_DIM = 64
ATTN_WIDTH = ATTN_HEADS * ATTN_HEAD_DIM
RNN_WIDTH = D_MODEL
RNN_BLOCK_WIDTH = 256
RNN_BLOCKS = RNN_WIDTH // RNN_BLOCK_WIDTH
CONV_WIDTH = 4
LRU_C = 8.0
DEEPNORM_ALPHA = (2.0 * DEPTH) ** 0.25
LN_EPS = 1e-5
LOG2_E = 1.4426950408889634
F32_MIN_NORMAL = float(np.finfo(np.float32).tiny)

LANES = 128
SUBLANES = 8
HEADS_PER_BLOCK = LANES // ATTN_HEAD_DIM
HEAD_BLOCKS = ATTN_HEADS // HEADS_PER_BLOCK
AUG_STRIDE = 8
CUM_PARTS = 3
ONES_LANE = ATTN_HEADS

VMEM_LIMIT = 56 * 1024 * 1024

PROJ_ROWS = 512
RNN_ROWS = 256
RNN_BATCH_ROWS = 2
RNN_STAGES = 6
RNN_STAGE_SKEW = 1
ATTN_TK = 256
ATTN_TQ = 4 * ATTN_TK
ATTN_HEAD_BLOCKS_PER_STEP = 2
BF16_SUBLANES = 2 * SUBLANES
ACC_ROWS = ATTN_HEAD_DIM + BF16_SUBLANES


def _softplus(x):
    return jnp.maximum(x, 0.0) + jnp.log1p(jnp.exp(-jnp.abs(x)))


def _log_sigmoid(x):
    return jnp.minimum(x, 0.0) - jnp.log1p(jnp.exp(-jnp.abs(x)))


def _split_bf16(x):
    hi = x.astype(BF16)
    r1 = x - hi.astype(F32)
    mid = r1.astype(BF16)
    lo = (r1 - mid.astype(F32)).astype(BF16)
    return hi, mid, lo


def _deepnorm_ln(x, h, g, b):
    z = DEEPNORM_ALPHA * x + h
    mu = jnp.mean(z, axis=-1, keepdims=True)
    zc = z - mu
    var = jnp.mean(zc * zc, axis=-1, keepdims=True)
    return zc * lax.rsqrt(var + LN_EPS) * g + b


def _dot_nt(a, b):
    return lax.dot_general(a, b, (((1,), (1,)), ((), ())), preferred_element_type=F32)


def _attn_proj_kernel(x_ref, wkf_ref, wg_ref, wqt_ref, wvt_ref, bf_ref, eqt_ref, ek_ref,
                      qt_ref, qat_ref, k_ref, ka_ref, vt_ref, g_ref, carry_ref, *, tm, tk):
    si = pl.program_id(1)

    @pl.when(si == 0)
    def _():
        carry_ref[...] = jnp.zeros_like(carry_ref)

    xb = x_ref[0].astype(BF16)
    kf = jnp.dot(xb, wkf_ref[...], preferred_element_type=F32)
    k_ref[0] = kf[:, :ATTN_WIDTH].astype(BF16)
    g_ref[0] = jnp.dot(xb, wg_ref[...], preferred_element_type=F32).astype(BF16)
    qt_ref[0] = (_dot_nt(wqt_ref[...], xb) * (ATTN_HEAD_DIM ** -0.5 * LOG2_E)).astype(BF16)
    vt = _dot_nt(wvt_ref[...], xb).astype(BF16)
    for hb in range(HEAD_BLOCKS):
        for c in range(tm // tk):
            vt_ref[0, hb, c] = vt[hb * LANES:(hb + 1) * LANES, c * tk:(c + 1) * tk]

    cum = _log_sigmoid(kf[:, ATTN_WIDTH:] + bf_ref[...])
    row = lax.broadcasted_iota(jnp.int32, cum.shape, 0)
    shift = 1
    while shift < tm:
        cum = cum + jnp.where(row >= shift, pltpu.roll(cum, shift, 0), 0.0)
        shift *= 2
    cum = cum + carry_ref[0:1, :]
    carry_ref[0:1, :] = cum[tm - 1:tm, :]

    hi, mid, lo = _split_bf16(cum * LOG2_E)
    lane = lax.broadcasted_iota(jnp.int32, hi.shape, 1)
    hi = jnp.where(lane == ONES_LANE, jnp.ones_like(hi), hi)
    parts = jnp.concatenate([hi, mid, lo], axis=1)
    qat_ref[0] = _dot_nt(eqt_ref[...], parts).astype(BF16)
    ka_ref[0] = jnp.dot(parts, ek_ref[...], preferred_element_type=F32).astype(BF16)


def _bias_expanders():
    assert ATTN_HEADS * AUG_STRIDE == LANES
    eq = np.zeros((CUM_PARTS * LANES, LANES), np.float32)
    ek = np.zeros_like(eq)
    for h in range(ATTN_HEADS):
        base = h * AUG_STRIDE
        for p in range(CUM_PARTS):
            eq[p * LANES + h, base + p] = 1.0
            ek[ONES_LANE, base + p] = 1.0
            ek[p * LANES + h, base + CUM_PARTS + p] = -1.0
            eq[ONES_LANE, base + CUM_PARTS + p] = 1.0
    return jnp.asarray(eq.T, BF16), jnp.asarray(ek, BF16)


def _attn_proj(x, w_kf, w_g, w_qt, w_vt, b_f):
    B, S, D = x.shape
    tm, tk = PROJ_ROWS, ATTN_TK
    eqt, ek = _bias_expanders()
    consts = [w_kf, w_g, w_qt, w_vt, b_f, eqt, ek]
    const = lambda b, s: (0, 0)
    row_blk = lambda b, s: (b, s, 0)
    col_blk = lambda b, s: (b, 0, s)
    s_major = pl.BlockSpec((1, tm, ATTN_WIDTH), row_blk)
    f_major = pl.BlockSpec((1, ATTN_WIDTH, tm), col_blk)
    vt_blk = pl.BlockSpec((1, HEAD_BLOCKS, tm // tk, LANES, tk), lambda b, s: (b, 0, s, 0, 0))
    s_shape = jax.ShapeDtypeStruct((B, S, ATTN_WIDTH), BF16)
    f_shape = jax.ShapeDtypeStruct((B, ATTN_WIDTH, S), BF16)
    vt_shape = jax.ShapeDtypeStruct((B, HEAD_BLOCKS, S // tk, LANES, tk), BF16)
    s_slots = pl.BlockSpec((1, tm, LANES), row_blk)
    f_slots = pl.BlockSpec((1, LANES, tm), col_blk)
    s_slots_shape = jax.ShapeDtypeStruct((B, S, LANES), BF16)
    f_slots_shape = jax.ShapeDtypeStruct((B, LANES, S), BF16)
    return pl.pallas_call(
        functools.partial(_attn_proj_kernel, tm=tm, tk=tk),
        grid=(B, S // tm),
        in_specs=[pl.BlockSpec((1, tm, D), row_blk)]
        + [pl.BlockSpec(c.shape, const, pipeline_mode=pl.Buffered(1)) for c in consts],
        out_specs=[f_major, f_slots, s_major, s_slots, vt_blk, s_major],
        out_shape=[f_shape, f_slots_shape, s_shape, s_slots_shape, vt_shape, s_shape],
        scratch_shapes=[pltpu.VMEM((SUBLANES, LANES), F32)],
        compiler_params=pltpu.CompilerParams(
            dimension_semantics=("arbitrary", "arbitrary"), vmem_limit_bytes=VMEM_LIMIT),
        name="attn_proj",
    )(x, *consts)


def _flash_kernel(qt_ref, qat_ref, k_ref, ka_ref, vt_ref, o_ref, s_scr, p_scr, acc_scr, *, tq, tk, hb):
    i = pl.program_id(2)
    ratio = tq // tk
    n_visits = ratio * i + ratio
    n_heads = hb * HEADS_PER_BLOCK
    qrow = lax.broadcasted_iota(jnp.int32, (2 * LANES, tq), 0)
    ones = jnp.ones((ACC_ROWS - ATTN_HEAD_DIM, tk), BF16)

    first_head = pl.program_id(1) * n_heads
    ka_all = ka_ref.at[0]
    q_heads = []
    for c in range(hb):
        q_all = jnp.concatenate([qt_ref[0, c * LANES:(c + 1) * LANES], qat_ref[0]], axis=0)
        for j in range(HEADS_PER_BLOCK):
            lo = j * ATTN_HEAD_DIM
            slot = LANES + (first_head + c * HEADS_PER_BLOCK + j) * AUG_STRIDE
            mine = ((qrow >= lo) & (qrow < lo + ATTN_HEAD_DIM)) | ((qrow >= slot) & (qrow < slot + AUG_STRIDE))
            q_heads.append(jnp.where(mine, q_all, jnp.zeros_like(q_all)))

    def block_of(visit, diagonal):
        return ratio * i + visit if diagonal else ratio * i + ratio - 1 - visit

    def logits(visit, diagonal=False):
        lo = visit * tk if diagonal else 0
        ks = pl.multiple_of(block_of(visit, diagonal) * tk, tk)
        maxima = []
        ka = ka_all[pl.ds(ks, tk), :]
        for c in range(hb):
            kk = jnp.concatenate([k_ref[0, pl.ds(ks, tk), c * LANES:(c + 1) * LANES], ka], axis=1)
            for j in range(HEADS_PER_BLOCK):
                h = c * HEADS_PER_BLOCK + j
                s = jnp.dot(kk, q_heads[h][:, lo:], preferred_element_type=F32)
                if diagonal:
                    key = lax.broadcasted_iota(jnp.int32, s.shape, 0)
                    qry = lax.broadcasted_iota(jnp.int32, s.shape, 1)
                    s = jnp.where(key <= qry, s, -jnp.inf)
                s_scr[h, :, lo:] = s
                maxima.append(jnp.broadcast_to(jnp.max(s, axis=0, keepdims=True), (SUBLANES, s.shape[1])))
        return tuple(maxima)

    def softmax(m, block_max, lo=0):
        m_out, alpha_out = [], []
        for h in range(n_heads):
            m_old = m[h][:, lo:]
            m_new = jnp.maximum(m_old, block_max[h])
            alpha_out.append(jnp.exp2(m_old - m_new))
            p_scr[h, :, lo:] = jnp.exp2(s_scr[h, :, lo:] - m_new[0:1]).astype(BF16)
            m_out.append(m_new if lo == 0 else jnp.concatenate([m[h][:, :lo], m_new], axis=1))
        return tuple(m_out), tuple(alpha_out)

    def accumulate(visit, alpha, diagonal=False):
        lo = visit * tk if diagonal else 0
        kb = block_of(visit, diagonal)
        for c in range(hb):
            vt = vt_ref[0, c, kb]
            for j in range(HEADS_PER_BLOCK):
                h = c * HEADS_PER_BLOCK + j
                v1 = jnp.concatenate([vt[j * ATTN_HEAD_DIM:(j + 1) * ATTN_HEAD_DIM], ones], axis=0)
                acc_scr[h, :, lo:] = alpha[h][0:1] * acc_scr[h, :, lo:] + jnp.dot(
                    v1, p_scr[h, :, lo:], preferred_element_type=F32)

    acc_scr[...] = jnp.zeros_like(acc_scr)
    m = tuple(jnp.full((SUBLANES, tq), -jnp.inf, F32) for _ in range(n_heads))

    block_max = logits(0, diagonal=True)
    m, alpha = softmax(m, block_max)
    block_max = logits(1, diagonal=True)
    for v in range(1, ratio - 1):
        accumulate(v - 1, alpha, diagonal=True)
        m, alpha = softmax(m, block_max, lo=v * tk)
        block_max = logits(v + 1, diagonal=True)
    last = ratio - 1

    @pl.when(i == 0)
    def _():
        accumulate(last - 1, alpha, diagonal=True)
        _, alpha_last = softmax(m, block_max, lo=last * tk)
        accumulate(last, alpha_last, diagonal=True)

    @pl.when(i > 0)
    def _():
        accumulate(last - 1, alpha, diagonal=True)
        m1, alpha1 = softmax(m, block_max, lo=last * tk)
        block_max1 = logits(ratio)
        accumulate(last, alpha1, diagonal=True)
        m2, alpha2 = softmax(m1, block_max1)
        carry = (m2, alpha2, logits(ratio + 1))

        def step(visit, carry):
            m, alpha_prev, block_max = carry
            accumulate(visit - 1, alpha_prev)
            m, alpha = softmax(m, block_max)
            return m, alpha, logits(visit + 1)

        def step_pair(t, carry):
            visit = ratio + 1 + 2 * t
            return step(visit + 1, step(visit, carry))

        m3, alpha3, block_max3 = lax.fori_loop(0, (ratio * i - 2) // 2, step_pair, carry)
        accumulate(n_visits - 2, alpha3)
        _, alpha_last = softmax(m3, block_max3)
        accumulate(n_visits - 1, alpha_last)

    d = ATTN_HEAD_DIM
    for c in range(hb):
        pair = []
        for j in range(HEADS_PER_BLOCK):
            acc = acc_scr[c * HEADS_PER_BLOCK + j]
            pair.append(acc[:d] / acc[d:d + 1])
        o_ref[0, :, c * LANES:(c + 1) * LANES] = jnp.concatenate(pair, axis=0).T.astype(BF16)


def _flash(qt, qat, k, ka, vt):
    B, S, _ = k.shape
    tq, tk, hb = ATTN_TQ, ATTN_TK, ATTN_HEAD_BLOCKS_PER_STEP
    assert tq % (2 * tk) == 0 and S % tq == 0
    n_heads = hb * HEADS_PER_BLOCK
    q_blk = pl.BlockSpec((1, hb * LANES, tq), lambda b, h, i: (b, h, i))
    q_slots = pl.BlockSpec((1, LANES, tq), lambda b, h, i: (b, 0, i))
    k_blk = pl.BlockSpec((1, S, hb * LANES), lambda b, h, i: (b, 0, h))
    k_slots = pl.BlockSpec((1, S, LANES), lambda b, h, i: (b, 0, 0))
    v_blk = pl.BlockSpec((1, hb, S // tk, LANES, tk), lambda b, h, i: (b, h, 0, 0, 0))
    return pl.pallas_call(
        functools.partial(_flash_kernel, tq=tq, tk=tk, hb=hb),
        grid=(B, HEAD_BLOCKS // hb, S // tq),
        in_specs=[q_blk, q_slots, k_blk, k_slots, v_blk],
        out_specs=pl.BlockSpec((1, tq, hb * LANES), lambda b, h, i: (b, i, h)),
        out_shape=jax.ShapeDtypeStruct((B, S, ATTN_WIDTH), BF16),
        scratch_shapes=[
            pltpu.VMEM((n_heads, tk, tq), F32),
            pltpu.VMEM((n_heads, tk, tq), BF16),
            pltpu.VMEM((n_heads, ACC_ROWS, tq), F32),
        ],
        compiler_params=pltpu.CompilerParams(
            dimension_semantics=("arbitrary", "arbitrary", "arbitrary"), vmem_limit_bytes=VMEM_LIMIT),
        name="fox_flash",
    )(qt, qat, k, ka, vt)


def _chunk_interleave(tm):
    steps = tm // SUBLANES
    p = np.zeros((tm, tm), np.float32)
    for s in range(SUBLANES):
        for r in range(steps):
            p[r * SUBLANES + s, s * steps + r] = 1.0
    return p


def _rnn_kernel(x_ref, ao_ref, ag_ref, *refs, tm, nb):
    consts, (y_ref, tail_ref, h_ref) = refs[:-3], refs[-3:]
    si = pl.program_id(1)

    @pl.when(si == 0)
    def _():
        tail_ref[...] = jnp.zeros_like(tail_ref)
        h_ref[...] = jnp.zeros_like(h_ref)

    tiles = [_rnn_tile(x_ref.at[bb], ao_ref.at[bb], ag_ref.at[bb], *consts,
                       y_ref.at[bb], tail_ref.at[bb], h_ref.at[bb], tm=tm)
             for bb in range(nb)]
    order = sorted(((RNN_STAGE_SKEW * bb + stage, -bb, bb) for bb in range(nb) for stage in range(RNN_STAGES)))
    for _, _, bb in order:
        next(tiles[bb], None)


def _rnn_tile(x_ref, ao_ref, ag_ref, aw_ref, alg_ref, alb_ref, perm_ref, unperm_ref, win_ref, cw_ref, cb_ref,
              wa_ref, ba_ref, wi_ref, bi_ref, lam_ref, wout_ref, lg_ref, lb_ref, y_ref, tail_ref, h_ref, *, tm):
    w = RNN_WIDTH
    steps = tm // SUBLANES
    ag_half = ag_ref[...].astype(F32)
    attn = ao_ref[...].astype(F32) * (ag_half * (jnp.tanh(ag_half) + 1.0))
    attn = jnp.dot(attn.astype(BF16), aw_ref[...], preferred_element_type=F32)
    x = _deepnorm_ln(x_ref[...], attn, alg_ref[...], alb_ref[...])
    yield
    xb = jnp.dot(perm_ref[...], x.astype(BF16), preferred_element_type=F32).astype(BF16)
    u = jnp.dot(xb, win_ref[:, 0:w], preferred_element_type=F32)
    gate_half = jnp.dot(xb, win_ref[:, w:2 * w], preferred_element_type=F32)
    yield

    n_tail = (CONV_WIDTH - 1) * SUBLANES
    row8 = lax.broadcasted_iota(jnp.int32, (SUBLANES, w), 0)
    wrapped = []
    for g in range(CONV_WIDTH - 1):
        lo = tm - n_tail + g * SUBLANES
        here = pltpu.roll(u[lo:lo + SUBLANES], 1, 0)
        before = pltpu.roll(tail_ref[g * SUBLANES:(g + 1) * SUBLANES, :], 1, 0)
        wrapped.append(jnp.where(row8 == 0, before, here))
    conv_half = cb_ref[...] + u * cw_ref[CONV_WIDTH - 1:CONV_WIDTH, :]
    for back in range(1, CONV_WIDTH):
        shifted = jnp.concatenate(wrapped[CONV_WIDTH - 1 - back:] + [u[:tm - back * SUBLANES]], axis=0)
        tap = CONV_WIDTH - 1 - back
        conv_half = conv_half + shifted * cw_ref[tap:tap + 1, :]
    tail_ref[...] = u[tm - n_tail:tm]

    cbf = conv_half.astype(BF16)
    bw = RNN_BLOCK_WIDTH
    ra = jnp.concatenate(
        [jnp.dot(cbf[:, n * bw:(n + 1) * bw], wa_ref[n], preferred_element_type=F32) for n in range(RNN_BLOCKS)],
        axis=1)
    ri = jnp.concatenate(
        [jnp.dot(cbf[:, n * bw:(n + 1) * bw], wi_ref[n], preferred_element_type=F32) for n in range(RNN_BLOCKS)],
        axis=1)
    yield
    tanh_r = jnp.tanh(ra + ba_ref[...])
    tanh_i = jnp.tanh(ri + bi_ref[...])
    c_half = (-0.5 * LRU_C) * _softplus(-lam_ref[...])
    log_a = c_half * tanh_r + c_half
    a = jnp.exp(log_a)
    gain2 = jnp.tanh(-log_a) * (a * a + 1.0)
    gain = gain2 * lax.rsqrt(jnp.maximum(gain2, F32_MIN_NORMAL))
    b = gain * ((tanh_i + 1.0) * conv_half)

    h_loc = jnp.zeros((SUBLANES, w), F32)
    a_cum = jnp.ones((SUBLANES, w), F32)
    h_locs, a_cums = [], []
    for step in range(steps):
        a_step = a[step * SUBLANES:(step + 1) * SUBLANES]
        h_loc = a_step * h_loc + b[step * SUBLANES:(step + 1) * SUBLANES]
        a_cum = a_step * a_cum
        h_locs.append(h_loc)
        a_cums.append(a_cum)
    state = h_ref[0:1, :]
    entering = []
    for s in range(SUBLANES):
        entering.append(state)
        state = a_cum[s:s + 1] * state + h_loc[s:s + 1]
    h_ref[0:1, :] = state
    h_in = jnp.concatenate(entering, axis=0)
    h = jnp.concatenate([h_locs[step] + a_cums[step] * h_in for step in range(steps)], axis=0)

    y = (h * (gate_half * (jnp.tanh(gate_half) + 1.0))).astype(BF16)
    yield
    y = jnp.dot(unperm_ref[...], y, preferred_element_type=F32).astype(BF16)
    out = jnp.dot(y, wout_ref[...], preferred_element_type=F32)
    yield
    y_ref[...] = _deepnorm_ln(x, out, lg_ref[...], lb_ref[...])


def _attn_out_rnn_layer(x, attn_o, attn_gate_half, attn_w_out, attn_ln_g, attn_ln_b,
                        w_in, conv_w, conv_b, w_a, b_a, w_i, b_i, lam, w_out, ln_g, ln_b):
    B, S, D = x.shape
    tm = RNN_ROWS
    row_blk = lambda b, s: (b, s, 0)

    def const(arr):
        nd = arr.ndim
        return pl.BlockSpec(arr.shape, lambda b, s: (0,) * nd, pipeline_mode=pl.Buffered(1))

    perm = _chunk_interleave(tm)
    consts = [attn_w_out, attn_ln_g, attn_ln_b, jnp.asarray(perm, BF16), jnp.asarray(perm.T, BF16),
              w_in, conv_w, conv_b, w_a, b_a, w_i, b_i, lam, w_out, ln_g, ln_b]
    nb = RNN_BATCH_ROWS
    assert B % nb == 0
    rows = [x, attn_o, attn_gate_half]
    return pl.pallas_call(
        functools.partial(_rnn_kernel, tm=tm, nb=nb),
        grid=(B // nb, S // tm),
        in_specs=[pl.BlockSpec((nb, tm, r.shape[-1]), row_blk) for r in rows] + [const(c) for c in consts],
        out_specs=pl.BlockSpec((nb, tm, D), row_blk),
        out_shape=jax.ShapeDtypeStruct((B, S, D), F32),
        scratch_shapes=[pltpu.VMEM((nb, (CONV_WIDTH - 1) * SUBLANES, RNN_WIDTH), F32),
                        pltpu.VMEM((nb, SUBLANES, RNN_WIDTH), F32)],
        compiler_params=pltpu.CompilerParams(
            dimension_semantics=("arbitrary", "arbitrary"), vmem_limit_bytes=VMEM_LIMIT),
        name="rglru_layer",
    )(*rows, *consts)


def _row(v):
    return v.reshape(1, -1).astype(F32)


def kernel(x, ln_g, ln_b, attn_w_in, attn_b_f, attn_w_out, rnn_w_in, rnn_conv_w, rnn_conv_b,
           rnn_w_a, rnn_b_a, rnn_w_i, rnn_b_i, rnn_lambda, rnn_w_out):
    assert x.shape[-1] == D_MODEL and x.shape[1] % ATTN_TQ == 0
    assert N_MIXERS == 2 and DEPTH % N_MIXERS == 0
    w = ATTN_WIDTH
    for layer in range(0, DEPTH, N_MIXERS):
        idx = layer // N_MIXERS
        w_in = attn_w_in[idx].astype(BF16)
        w_kf = jnp.pad(jnp.concatenate([w_in[:, w:2 * w], w_in[:, 4 * w:]], axis=1),
                       ((0, 0), (0, LANES - ATTN_HEADS)))
        b_f = jnp.pad(attn_b_f[idx].astype(F32), (0, LANES - ATTN_HEADS)).reshape(1, LANES)
        qt, qat, k, ka, vt, gate_half = _attn_proj(
            x, w_kf, 0.5 * w_in[:, 3 * w:4 * w], w_in[:, 0:w].T, w_in[:, 2 * w:3 * w].T, b_f)
        o = _flash(qt, qat, k, ka, vt)
        rnn_in = rnn_w_in[idx] * jnp.where(jnp.arange(2 * RNN_WIDTH) < RNN_WIDTH, 1.0, 0.5)
        x = _attn_out_rnn_layer(
            x, o, gate_half, attn_w_out[idx].astype(BF16), _row(ln_g[layer]), _row(ln_b[layer]),
            rnn_in.astype(BF16), 0.5 * rnn_conv_w[idx].astype(F32), 0.5 * _row(rnn_conv_b[idx]),
            rnn_w_a[idx].astype(BF16), 0.5 * _row(rnn_b_a[idx]),
            rnn_w_i[idx].astype(BF16), 0.5 * _row(rnn_b_i[idx]),
            _row(rnn_lambda[idx]), rnn_w_out[idx].astype(BF16), _row(ln_g[layer + 1]), _row(ln_b[layer + 1]))
    return x
```
